```python
import math, functools
import jax, jax.numpy as jnp
from jax import lax
import numpy as np

D_MODEL = 1024
BATCH = 2
SEQ = 8192
DEPTH = 1
DEC_BATCH = 128
DEC_SEQ = 4
PAST_LEN = 8192
PAGE_SIZE = 128

A_HEADS = 8
A_HEAD_DIM = 64
A_WIDTH = A_HEADS * A_HEAD_DIM
IDX_HEADS = 4
IDX_DIM = 64
INDEX_TOPK = 256
INDEX_SCALE = (IDX_HEADS * IDX_DIM) ** -0.5
Q_BLOCK = 128
REL_BUCKETS = 32
REL_MAX_DIST = 128
B_HEADS = 4
B_KEY_DIM = 64
B_VAL_DIM = 128
B_QK_WIDTH = B_HEADS * B_KEY_DIM
B_V_WIDTH = B_HEADS * B_VAL_DIM
GATE_RANK = 16
GATE_TAU = 16.0
GLA_CHUNK = 64
N_EXPERTS = 32
TOP_K = 4
D_EXPERT = D_MODEL
SWIGLU_ALPHA = 1.702
SWIGLU_LIMIT = 7.0
EPS = 1e-6
POOL_NUM = 5
POOL_DEN = 4
IN_SPLITS = (A_WIDTH, A_WIDTH, A_WIDTH, IDX_HEADS * IDX_DIM, IDX_DIM, IDX_HEADS,
             B_QK_WIDTH, B_QK_WIDTH, B_V_WIDTH, GATE_RANK, B_V_WIDTH, D_MODEL, D_MODEL)
IN_WIDTH = sum(IN_SPLITS)

kernel_name = "hybrid_dsa_gla_moe_step"


def rms_norm(x, g):
    xf = x.astype(jnp.float32)
    xf = xf * lax.rsqrt(jnp.mean(xf * xf, axis=-1, keepdims=True) + EPS)
    return (xf * g.astype(jnp.float32)).astype(x.dtype)


def split_heads(a, n):
    return a.reshape(*a.shape[:-1], n, a.shape[-1] // n)


def gather_rows(a, idx):
    return jax.vmap(lambda ab, ib: ab[ib])(a, idx)


def t5_bucket(dist):
    dist = jnp.maximum(dist, 0)
    max_exact = REL_BUCKETS // 2
    log_ratio = jnp.log(jnp.maximum(dist, 1).astype(jnp.float32) / max_exact) / math.log(REL_MAX_DIST / max_exact)
    large = jnp.minimum(max_exact + (log_ratio * (REL_BUCKETS - max_exact)).astype(jnp.int32), REL_BUCKETS - 1)
    return jnp.where(dist < max_exact, dist, large)


def index_scores(qi, wi, ki):
    s = jnp.einsum("bqhd,bsd->bqhs", qi.astype(jnp.float32), ki.astype(jnp.float32))
    return jnp.einsum("bqhs,bqh->bqs", jax.nn.relu(s), wi.astype(jnp.float32) * INDEX_SCALE)


def attend_selected(q, k_sel, v_sel, q_pos, k_idx, rel_table):
    dist = q_pos[None, :, None] - k_idx
    bias = jnp.swapaxes(rel_table[t5_bucket(dist)], -1, -2).astype(jnp.float32)
    logits = jnp.einsum("bqhd,bqkhd->bqhk", q.astype(jnp.float32), k_sel.astype(jnp.float32)) * A_HEAD_DIM ** -0.5 + bias
    logits = jnp.where((dist >= 0)[:, :, None, :], logits, -jnp.inf)
    p = jax.nn.softmax(logits, axis=-1)
    o = jnp.einsum("bqhk,bqkhd->bqhd", p, v_sel.astype(jnp.float32))
    return o.reshape(*o.shape[:2], A_WIDTH).astype(q.dtype)


def dsa_prompt(q, k, v, qi, ki, wi, rel_table):
    B, L = q.shape[:2]
    topk = min(INDEX_TOPK, L // 4)
    nb = L // Q_BLOCK
    kpos = jnp.arange(L)

    def blocks(a):
        return jnp.moveaxis(a.reshape(B, nb, Q_BLOCK, *a.shape[2:]), 1, 0)

    def one_block(args):
        qb, qib, wib, start = args
        pos = start + jnp.arange(Q_BLOCK)
        s = index_scores(qib, wib, ki)
        s = jnp.where(kpos[None, None, :] <= pos[None, :, None], s, -jnp.inf)
        _, idx = lax.top_k(s, topk)
        return attend_selected(qb, gather_rows(k, idx), gather_rows(v, idx), pos, idx, rel_table)

    starts = jnp.arange(nb, dtype=jnp.int32) * Q_BLOCK
    out = lax.map(one_block, (blocks(q), blocks(qi), blocks(wi), starts))
    return jnp.moveaxis(out, 0, 1).reshape(B, L, A_WIDTH)


def dsa_sample(q, k_new, v_new, qi, ki_new, wi, cache_k, cache_v, cache_ki, page_table, rel_table):
    DB, Q = q.shape[:2]
    n_pages = page_table.shape[1]
    past = n_pages * PAGE_SIZE
    L = past + Q
    topk = min(INDEX_TOPK, L // 4)
    pos = past + jnp.arange(Q)
    ki_past = cache_ki[page_table].reshape(DB, past, IDX_DIM)
    ki_all = jnp.concatenate([ki_past, ki_new.astype(ki_past.dtype)], axis=1)
    s = index_scores(qi, wi, ki_all)
    s = jnp.where(jnp.arange(L)[None, None, :] <= pos[None, :, None], s, -jnp.inf)
    _, idx = lax.top_k(s, topk)
    in_past = (idx < past)[..., None, None]
    page = jnp.minimum(idx // PAGE_SIZE, n_pages - 1)
    off = idx % PAGE_SIZE
    phys = jax.vmap(lambda pt, p: pt[p])(page_table, page)
    new_i = jnp.clip(idx - past, 0, Q - 1)
    k_sel = jnp.where(in_past, cache_k[phys, off], gather_rows(k_new, new_i).astype(cache_k.dtype))
    v_sel = jnp.where(in_past, cache_v[phys, off], gather_rows(v_new, new_i).astype(cache_v.dtype))
    return attend_selected(q, k_sel, v_sel, pos, idx, rel_table)


def gla_scan(q, k, v, log_a, S0, chunk):
    B, L, H, DK = q.shape
    DV = v.shape[-1]
    n = L // chunk
    tri = jnp.tril(jnp.ones((chunk, chunk), dtype=bool))

    def blocks(a):
        return jnp.moveaxis(a.reshape(B, n, chunk, *a.shape[2:]), 1, 0)

    def step(S, xs):
        qc, kc, vc, gc = xs
        b = jnp.cumsum(gc, axis=1)
        o_inter = jnp.einsum("bchk,bhkv->bchv", qc * jnp.exp(b), S)
        diff = b[:, :, None] - b[:, None, :]
        decay = jnp.where(tri[None, :, :, None, None], jnp.exp(jnp.minimum(diff, 0.0)), 0.0)
        att = jnp.einsum("bihk,bjhk,bijhk->bhij", qc, kc, decay)
        o_intra = jnp.einsum("bhij,bjhv->bihv", att, vc)
        b_last = b[:, -1]
        S = jnp.exp(b_last)[..., None] * S + jnp.einsum("bjhk,bjhv->bhkv", kc * jnp.exp(b_last[:, None] - b), vc)
        return S, o_inter + o_intra

    S, o = lax.scan(step, S0, (blocks(q), blocks(k), blocks(v), blocks(log_a)))
    return jnp.moveaxis(o, 0, 1).reshape(B, L, H, DV), S


def gla_branch(qb, kb, vb, glow, r, w_gate_up, b_gate, gla_norm, S0, chunk):
    B, L = qb.shape[:2]
    f32 = jnp.float32
    q = qb.reshape(B, L, B_HEADS, B_KEY_DIM).astype(f32) * B_KEY_DIM ** -0.5
    k = kb.reshape(B, L, B_HEADS, B_KEY_DIM).astype(f32)
    v = vb.reshape(B, L, B_HEADS, B_VAL_DIM).astype(f32)
    log_a = (jax.nn.log_sigmoid((glow @ w_gate_up + b_gate).astype(f32)) / GATE_TAU).reshape(B, L, B_HEADS, B_KEY_DIM)
    o, S = gla_scan(q, k, v, log_a, S0.astype(f32), chunk)
    o = o * lax.rsqrt(jnp.mean(o * o, axis=-1, keepdims=True) + EPS)
    o = o.reshape(B, L, B_V_WIDTH) * gla_norm.astype(f32) * jax.nn.silu(r.astype(f32))
    return o.astype(qb.dtype), S


def moe(h, w_router, b_router, w_exp_up, b_exp_up, w_exp_down, b_exp_down):
    B, L, D = h.shape
    t = h.reshape(B * L, D)
    logits = (t @ w_router + b_router).astype(jnp.float32)
    vals, idx = lax.top_k(logits, TOP_K)
    wts = jax.nn.softmax(vals, axis=-1)
    combine = jnp.sum(jax.nn.one_hot(idx, N_EXPERTS, dtype=jnp.float32) * wts[..., None], axis=1)

    def expert(acc, p):
        wu, bu, wd, bd, c = p
        z = t @ wu + bu
        glu, lin = z[:, :D_EXPERT], z[:, D_EXPERT:]
        glu = jnp.minimum(glu, SWIGLU_LIMIT)
        lin = jnp.clip(lin, -SWIGLU_LIMIT, SWIGLU_LIMIT)
        a = glu * jax.nn.sigmoid(SWIGLU_ALPHA * glu) * (lin + 1.0)
        y = a @ wd + bd
        return acc + c[:, None].astype(y.dtype) * y, None

    acc, _ = lax.scan(expert, jnp.zeros_like(t), (w_exp_up, b_exp_up, w_exp_down, b_exp_down, combine.T))
    return acc.reshape(B, L, D)


def trunk_layer(x, attend, S0, chunk, norm_mix, w_in, w_gate_up, b_gate, gla_norm, w_branch_a, w_branch_b,
                w_out, norm_ffn, w_router, b_router, w_exp_up, b_exp_up, w_exp_down, b_exp_down):
    h = rms_norm(x, norm_mix)
    split_points = np.cumsum(IN_SPLITS)[:-1].tolist()
    qa, ka, va, qi, ki, wi, qb, kb, vb, glow, r, ga, gb = jnp.split(h @ w_in, split_points, axis=-1)
    qa, ka, va = split_heads(qa, A_HEADS), split_heads(ka, A_HEADS), split_heads(va, A_HEADS)
    qi = split_heads(qi, IDX_HEADS)
    o_a = attend(qa, ka, va, qi, ki, wi)
    o_b, S = gla_branch(qb, kb, vb, glow, r, w_gate_up, b_gate, gla_norm, S0, chunk)
    merged = jax.nn.sigmoid(ga) * (o_a @ w_branch_a) + jax.nn.sigmoid(gb) * (o_b @ w_branch_b)
    x = x + merged @ w_out
    x = x + moe(rms_norm(x, norm_ffn), w_router, b_router, w_exp_up, b_exp_up, w_exp_down, b_exp_down)
    return x, ka, va, ki, S


def setup_inputs(seed: int = 0) -> dict:
    key = jax.random.key(seed)
    ks = jax.random.split(key, 24)
    f32 = jnp.float32
    n_pages = PAST_LEN // PAGE_SIZE
    n_pool = (DEC_BATCH * n_pages * POOL_NUM) // POOL_DEN

    def nrm(k, shape, scale=1.0):
        return jax.random.normal(k, shape, f32) * scale

    page_table = jax.random.permutation(ks[6], n_pool)[: DEC_BATCH * n_pages].reshape(DEC_BATCH, n_pages).astype(jnp.int32)
    return {
        "x_prompt": nrm(ks[0], (BATCH, SEQ, D_MODEL)),
        "x_sample": nrm(ks[1], (DEC_BATCH, DEC_SEQ, D_MODEL)),
        "cache_k": nrm(ks[2], (DEPTH, n_pool, PAGE_SIZE, A_HEADS, A_HEAD_DIM)),
        "cache_v": nrm(ks[3], (DEPTH, n_pool, PAGE_SIZE, A_HEADS, A_HEAD_DIM)),
        "cache_idx_k": nrm(ks[4], (DEPTH, n_pool, PAGE_SIZE, IDX_DIM)),
        "state_gla": nrm(ks[5], (DEPTH, DEC_BATCH, B_HEADS, B_KEY_DIM, B_VAL_DIM), 0.5),
        "page_table": page_table,
        "norm_mix": 1.0 + nrm(ks[7], (DEPTH, D_MODEL), 0.02),
        "w_in": nrm(ks[8], (DEPTH, D_MODEL, IN_WIDTH), D_MODEL ** -0.5),
        "w_gate_up": nrm(ks[9], (DEPTH, GATE_RANK, B_QK_WIDTH), GATE_RANK ** -0.5),
        "b_gate": nrm(ks[10], (DEPTH, B_QK_WIDTH), 0.1),
        "gla_norm": 1.0 + nrm(ks[11], (DEPTH, B_V_WIDTH), 0.02),
        "w_branch_a": nrm(ks[12], (DEPTH, A_WIDTH, D_MODEL), A_WIDTH ** -0.5),
        "w_branch_b": nrm(ks[13], (DEPTH, B_V_WIDTH, D_MODEL), B_V_WIDTH ** -0.5),
        "w_out": nrm(ks[14], (DEPTH, D_MODEL, D_MODEL), D_MODEL ** -0.5),
        "norm_ffn": 1.0 + nrm(ks[15], (DEPTH, D_MODEL), 0.02),
        "w_router": nrm(ks[16], (DEPTH, D_MODEL, N_EXPERTS), D_MODEL ** -0.5),
        "b_router": nrm(ks[17], (DEPTH, N_EXPERTS), 0.01),
        "w_exp_up": nrm(ks[18], (DEPTH, N_EXPERTS, D_MODEL, 2 * D_EXPERT), D_MODEL ** -0.5),
        "b_exp_up": nrm(ks[19], (DEPTH, N_EXPERTS, 2 * D_EXPERT), 0.01),
        "w_exp_down": nrm(ks[20], (DEPTH, N_EXPERTS, D_EXPERT, D_MODEL), D_EXPERT ** -0.5),
        "b_exp_down": nrm(ks[21], (DEPTH, N_EXPERTS, D_MODEL), 0.01),
        "rel_bias": nrm(ks[22], (REL_BUCKETS, A_HEADS), 0.5),
        "norm_final": 1.0 + nrm(ks[23], (D_MODEL,), 0.02),
    }


def reference(x_prompt, x_sample, cache_k, cache_v, cache_idx_k, state_gla, page_table, norm_mix, w_in, w_gate_up,
              b_gate, gla_norm, w_branch_a, w_branch_b, w_out, norm_ffn, w_router, b_router, w_exp_up, b_exp_up,
              w_exp_down, b_exp_down, rel_bias, norm_final):
    xp, xs = x_prompt, x_sample
    chunk_p = min(GLA_CHUNK, xp.shape[1])
    kp_l, vp_l, kip_l, sp_l, ks_l, vs_l, kis_l, ss_l = [], [], [], [], [], [], [], []
    for l in range(DEPTH):
        lw = (norm_mix[l], w_in[l], w_gate_up[l], b_gate[l], gla_norm[l], w_branch_a[l], w_branch_b[l], w_out[l],
              norm_ffn[l], w_router[l], b_router[l], w_exp_up[l], b_exp_up[l], w_exp_down[l], b_exp_down[l])
        attend_p = functools.partial(dsa_prompt, rel_table=rel_bias)
        s0 = jnp.zeros((xp.shape[0], B_HEADS, B_KEY_DIM, B_VAL_DIM), jnp.float32)
        xp, kp, vp, kip, sp = trunk_layer(xp, attend_p, s0, chunk_p, *lw)
        attend_s = functools.partial(dsa_sample, cache_k=cache_k[l], cache_v=cache_v[l], cache_ki=cache_idx_k[l],
                                     page_table=page_table, rel_table=rel_bias)
        xs, ks, vs, kis, ss = trunk_layer(xs, attend_s, state_gla[l], xs.shape[1], *lw)
        kp_l.append(kp); vp_l.append(vp); kip_l.append(kip); sp_l.append(sp.astype(x_prompt.dtype))
        ks_l.append(ks); vs_l.append(vs); kis_l.append(kis); ss_l.append(ss.astype(state_gla.dtype))
    y_prompt = rms_norm(xp, norm_final)
    y_sample = rms_norm(xs, norm_final)
    return (y_prompt, y_sample, jnp.stack(kp_l), jnp.stack(vp_l), jnp.stack(kip_l), jnp.stack(sp_l),
            jnp.stack(ks_l), jnp.stack(vs_l), jnp.stack(kis_l), jnp.stack(ss_l))
```

```python
import functools
import math

import numpy as np
import jax
import jax.numpy as jnp
from jax import lax
from jax.experimental import pallas as pl
from jax.experimental.pallas import tpu as pltpu

F32 = jnp.float32
BF16 = jnp.bfloat16

A_HEADS = 8
A_HEAD_DIM = 64
A_WIDTH = A_HEADS * A_HEAD_DIM
IDX_HEADS = 4
IDX_DIM = 64
INDEX_TOPK = 256
INDEX_SCALE = (IDX_HEADS * IDX_DIM) ** -0.5
REL_BUCKETS = 32
REL_MAX_DIST = 128
B_HEADS = 4
B_KEY_DIM = 64
B_VAL_DIM = 128
B_QK_WIDTH = B_HEADS * B_KEY_DIM
B_V_WIDTH = B_HEADS * B_VAL_DIM
GATE_RANK = 16
GATE_TAU = 16.0
GLA_CHUNK = 64
TOP_K = 4
SWIGLU_ALPHA = 1.702
SWIGLU_LIMIT = 7.0
EPS = 1e-6
PAGE_SIZE = 128

LANES = 128
NEG_BIG = -1e30
INT_MIN = -2 ** 31
VMEM_LIMIT = 56 * 1024 * 1024


def _cparams(sem):
    return pltpu.CompilerParams(dimension_semantics=sem, vmem_limit_bytes=VMEM_LIMIT)


def _rms(xf, g):
    return xf * lax.rsqrt(jnp.mean(xf * xf, axis=-1, keepdims=True) + EPS) * g


_C_QA, _C_KA, _C_VA = 0, 512, 1024
_C_QI, _C_KIW = 1536, 1792
_C_QB, _C_KB, _C_VB, _C_R, _C_GLOW = 1920, 2176, 2432, 2944, 3456
_C_GA, _C_GB = 3584, 4608
_C_END = 5632


def _pack_w_in(w_in):
    d = w_in.shape[0]
    o = np.cumsum([0, 512, 512, 512, 256, 64, 4, 256, 256, 512, 16, 512, 1024, 1024]).tolist()
    seg = lambda i: w_in[:, o[i]:o[i + 1]]
    z = lambda n: jnp.zeros((d, n), w_in.dtype)
    packed = jnp.concatenate(
        [seg(0), seg(1), seg(2), seg(3), seg(4), seg(5), z(60),
         seg(6), seg(7), seg(8), seg(10), seg(9), z(112), seg(11), seg(12)], axis=1)
    assert packed.shape[1] == _C_END
    return packed.astype(BF16)


def _inproj_kernel(x_ref, g_ref, w_ref, qa_ref, ka_ref, kab_ref, va_ref, vab_ref, qi_ref, kiw_ref,
                   qb_ref, kb_ref, vb_ref, r_ref, glow_ref, ga_ref, gb_ref):
    h = _rms(x_ref[...], g_ref[...]).astype(BF16)

    def proj(a, b):
        return jnp.dot(h, w_ref[:, a:b], preferred_element_type=F32)

    za = proj(_C_QA, _C_QI)
    qa_ref[...] = (za[:, 0:512] * (A_HEAD_DIM ** -0.5)).astype(BF16)
    ka = za[:, 512:1024]
    va = za[:, 1024:1536]
    ka_ref[...] = ka
    kab_ref[...] = ka.astype(BF16)
    va_ref[...] = va
    vab_ref[...] = va.astype(BF16)
    zi = proj(_C_QI, _C_QB)
    qi_ref[...] = zi[:, 0:256].astype(BF16)
    kiw_ref[...] = zi[:, 256:384]
    zb = proj(_C_QB, _C_GA)
    qb_ref[...] = zb[:, 0:256]
    kb_ref[...] = zb[:, 256:512]
    vb_ref[...] = zb[:, 512:1024]
    r_ref[...] = zb[:, 1024:1536]
    glow_ref[...] = zb[:, 1536:1664]
    zg = proj(_C_GA, _C_END)
    ga_ref[...] = zg[:, 0:1024]
    gb_ref[...] = zg[:, 1024:2048]


def _inproj(x2d, g, w_packed, tm):
    t, d = x2d.shape
    widths = [(512, BF16), (512, F32), (512, BF16), (512, F32), (512, BF16), (256, BF16), (128, F32),
              (256, F32), (256, F32), (512, F32), (512, F32), (128, F32), (1024, F32), (1024, F32)]
    return pl.pallas_call(
        _inproj_kernel,
        grid=(t // tm,),
        in_specs=[pl.BlockSpec((tm, d), lambda i: (i, 0)),
                  pl.BlockSpec((1, d), lambda i: (0, 0)),
                  pl.BlockSpec((d, _C_END), lambda i: (0, 0))],
        out_specs=[pl.BlockSpec((tm, w), lambda i: (i, 0)) for w, _ in widths],
        out_shape=[jax.ShapeDtypeStruct((t, w), dt) for w, dt in widths],
        compiler_params=_cparams(("arbitrary",)),
        name="inproj",
    )(x2d, g.reshape(1, d), w_packed)


def _sortable_key(s):
    bits = pltpu.bitcast(s, jnp.int32)
    key = bits ^ ((bits >> 31) & 0x7FFFFFFF)
    return jnp.where(key == -1, 0, key)


def _bias_by_distance(rel_bias, n):
    dist = jnp.arange(n, dtype=jnp.int32)
    max_exact = REL_BUCKETS // 2
    log_ratio = jnp.log(jnp.maximum(dist, 1).astype(F32) / max_exact) / math.log(REL_MAX_DIST / max_exact)
    large = jnp.minimum(max_exact + (log_ratio * (REL_BUCKETS - max_exact)).astype(jnp.int32), REL_BUCKETS - 1)
    bucket = jnp.where(dist < max_exact, dist, large)
    return rel_bias[bucket].T.astype(F32)


def _dsa_prompt_kernel(qT_ref, qiT_ref, wT_ref, k_ref, vT_ref, ki_ref, bias_ref, o_ref,
                       s_scr, qbd_scr, acc_scr, m_scr, l_scr, *, tq, tk, topk):
    i = pl.program_id(1)
    nkb = (i * tq + tq + tk - 1) // tk
    qpos = i * tq + lax.broadcasted_iota(jnp.int32, (1, tq), 1)
    krow = lax.broadcasted_iota(jnp.int32, (tk, 1), 0)

    qiT = qiT_ref[0]
    qi_st = jnp.concatenate([qiT[h * IDX_DIM:(h + 1) * IDX_DIM, :] for h in range(IDX_HEADS)], axis=1)
    w = wT_ref[0]

    def score_block(j, carry):
        s = jnp.dot(ki_ref[0, j], qi_st, preferred_element_type=F32)
        acc = jnp.maximum(s[:, 0:tq], 0.0) * w[0:1, :]
        for h in range(1, IDX_HEADS):
            acc = acc + jnp.maximum(s[:, h * tq:(h + 1) * tq], 0.0) * w[h:h + 1, :]
        acc = jnp.where(j * tk + krow <= qpos, acc, -jnp.inf)
        s_scr[pl.ds(pl.multiple_of(j * tk, tk), tk), :] = _sortable_key(acc)
        return carry

    lax.fori_loop(0, nkb, score_block, 0)

    def count_ge(cand):
        def body(j, c):
            blk = s_scr[pl.ds(pl.multiple_of(j * tk, tk), tk), :]
            hit = jnp.where(blk >= cand, 1, 0).astype(jnp.int32)
            return c + jnp.sum(hit.reshape(tk // 8, 8, tq), axis=0)
        c = lax.fori_loop(0, nkb, body, jnp.zeros((8, tq), jnp.int32))
        return jnp.sum(c, axis=0, keepdims=True)

    def bisect(step, thr):
        cand = thr + lax.shift_left(jnp.int32(1), 31 - step)
        return jnp.where(count_ge(cand) >= topk, cand, thr)

    thr = lax.fori_loop(0, 32, bisect, jnp.full((1, tq), INT_MIN, jnp.int32))
    n_take = (topk - count_ge(thr + 1)).astype(F32)

    rowid = lax.broadcasted_iota(jnp.int32, (2 * A_HEAD_DIM, tq), 0)
    for p in range(A_HEADS // 2):
        slab = qT_ref[0, p * 128:(p + 1) * 128, :].astype(F32)
        top = jnp.where(rowid < A_HEAD_DIM, slab, 0.0)
        qbd_scr[p] = jnp.concatenate([top, slab - top], axis=1).astype(BF16)

    acc_scr[...] = jnp.zeros_like(acc_scr)
    m_scr[...] = jnp.full_like(m_scr, NEG_BIG)
    l_scr[...] = jnp.zeros_like(l_scr)
    ri = lax.broadcasted_iota(jnp.int32, (tk, tk), 0)
    ci = lax.broadcasted_iota(jnp.int32, (tk, tk), 1)
    ltri = jnp.where(ci < ri, 1.0, 0.0).astype(BF16)

    def attend_block(j, tie_seen):
        keyblk = s_scr[pl.ds(pl.multiple_of(j * tk, tk), tk), :]
        eq = jnp.where(keyblk == thr, 1.0, 0.0)
        before = tie_seen + jnp.dot(ltri, eq.astype(BF16), preferred_element_type=F32)
        take = jnp.where(keyblk > thr, 1.0, jnp.where(before < n_take, eq, 0.0))
        take = jnp.where(j * tk + krow <= qpos, take, 0.0)
        sel = take > 0.5
        kind = jnp.minimum((i * tq) // tk - j, 2)
        for p in range(A_HEADS // 2):
            lg = jnp.dot(k_ref[0, j, :, p * 128:(p + 1) * 128], qbd_scr[p], preferred_element_type=F32)
            for u in range(2):
                h = 2 * p + u
                x = jnp.where(sel, lg[:, u * tq:(u + 1) * tq] + bias_ref[kind, h], NEG_BIG)
                m_old = m_scr[h:h + 1, :]
                m_new = jnp.maximum(m_old, jnp.max(x, axis=0, keepdims=True))
                alpha = jnp.exp(m_old - m_new)
                pm = jnp.exp(x - m_new)
                l_scr[h:h + 1, :] = alpha * l_scr[h:h + 1, :] + jnp.sum(pm, axis=0, keepdims=True)
                pv = jnp.dot(vT_ref[0, j, h * A_HEAD_DIM:(h + 1) * A_HEAD_DIM, :], pm.astype(BF16),
                             preferred_element_type=F32)
                rows = slice(h * A_HEAD_DIM, (h + 1) * A_HEAD_DIM)
                acc_scr[rows, :] = alpha * acc_scr[rows, :] + pv
                m_scr[h:h + 1, :] = m_new
        return tie_seen + jnp.sum(eq, axis=0, keepdims=True)

    lax.fori_loop(0, nkb, attend_block, jnp.zeros((1, tq), F32))

    for h in range(A_HEADS):
        rows = slice(h * A_HEAD_DIM, (h + 1) * A_HEAD_DIM)
        acc_scr[rows, :] = acc_scr[rows, :] / l_scr[h:h + 1, :]
    o_ref[0] = acc_scr[...].T.astype(o_ref.dtype)


def _dsa_prompt(qa_bf, ka_bf, va_bf, qi_bf, kiw, rel_bias, batch, seq):
    tq = tk = 256
    assert seq % tq == 0
    topk = min(INDEX_TOPK, seq // 4)
    nk = seq // tk
    qT = qa_bf.reshape(batch, seq, A_WIDTH).transpose(0, 2, 1)
    qiT = qi_bf.reshape(batch, seq, IDX_HEADS * IDX_DIM).transpose(0, 2, 1)
    wi = kiw[:, IDX_DIM:IDX_DIM + IDX_HEADS] * INDEX_SCALE
    wT = jnp.pad(wi.reshape(batch, seq, IDX_HEADS).transpose(0, 2, 1), ((0, 0), (0, 8 - IDX_HEADS), (0, 0)))
    k4 = ka_bf.reshape(batch, nk, tk, A_WIDTH)
    vT4 = va_bf.reshape(batch, nk, tk, A_WIDTH).transpose(0, 1, 3, 2)
    ki4 = kiw[:, :IDX_DIM].astype(BF16).reshape(batch, nk, tk, IDX_DIM)
    table = _bias_by_distance(rel_bias, 2 * tq + tk)
    d0 = jnp.arange(tq)[None, :] - jnp.arange(tk)[:, None]
    tiles = [table[:, jnp.clip(d0 + kind * tk, 0, table.shape[1] - 1)] for kind in range(3)]
    bias = jnp.stack(tiles).astype(F32)

    once = dict(pipeline_mode=pl.Buffered(1))
    kern = functools.partial(_dsa_prompt_kernel, tq=tq, tk=tk, topk=topk)
    out = pl.pallas_call(
        kern,
        grid=(batch, seq // tq),
        in_specs=[pl.BlockSpec((1, A_WIDTH, tq), lambda b, i: (b, 0, i)),
                  pl.BlockSpec((1, IDX_HEADS * IDX_DIM, tq), lambda b, i: (b, 0, i)),
                  pl.BlockSpec((1, 8, tq), lambda b, i: (b, 0, i)),
                  pl.BlockSpec((1, nk, tk, A_WIDTH), lambda b, i: (b, 0, 0, 0), **once),
                  pl.BlockSpec((1, nk, A_WIDTH, tk), lambda b, i: (b, 0, 0, 0), **once),
                  pl.BlockSpec((1, nk, tk, IDX_DIM), lambda b, i: (b, 0, 0, 0), **once),
                  pl.BlockSpec((3, A_HEADS, tk, tq), lambda b, i: (0, 0, 0, 0), **once)],
        out_specs=pl.BlockSpec((1, tq, A_WIDTH), lambda b, i: (b, i, 0)),
        out_shape=jax.ShapeDtypeStruct((batch, seq, A_WIDTH), BF16),
        scratch_shapes=[pltpu.VMEM((seq, tq), jnp.int32),
                        pltpu.VMEM((A_HEADS // 2, 2 * A_HEAD_DIM, 2 * tq), BF16),
                        pltpu.VMEM((A_WIDTH, tq), F32),
                        pltpu.VMEM((A_HEADS, tq), F32),
                        pltpu.VMEM((A_HEADS, tq), F32)],
        compiler_params=_cparams(("arbitrary", "arbitrary")),
        name="dsa_prompt",
    )(qT, qiT, wT, k4, vT4, ki4, bias)
    return out.reshape(batch * seq, A_WIDTH)


def _gla_kernel(q_ref, k_ref, v_ref, r_ref, glow_ref, wg_ref, bg_ref, gn_ref, s0_ref, o_ref, sT_ref,
                q_scr, b_scr, att_scr, *, n_valid):
    C = GLA_CHUNK
    hp = lax.Precision.HIGHEST

    @pl.when(pl.program_id(1) == 0)
    def _():
        sT_ref[...] = s0_ref[...]

    g = jnp.dot(glow_ref[0], wg_ref[...], precision=hp, preferred_element_type=F32) + bg_ref[...]
    log_a = (jnp.minimum(g, 0.0) - jnp.log1p(jnp.exp(-jnp.abs(g)))) * (1.0 / GATE_TAU)
    if n_valid < C:
        log_a = jnp.where(lax.broadcasted_iota(jnp.int32, (C, 1), 0) < n_valid, log_a, 0.0)
        att_scr[...] = jnp.zeros_like(att_scr)
    tri = jnp.where(lax.broadcasted_iota(jnp.int32, (C, C), 1) <= lax.broadcasted_iota(jnp.int32, (C, C), 0), 1.0, 0.0)
    b = jnp.dot(tri, log_a, precision=hp, preferred_element_type=F32)
    q = q_ref[0] * (B_KEY_DIM ** -0.5)
    k = k_ref[0]
    v = v_ref[0]
    q_scr[...] = q
    b_scr[...] = b
    eb = jnp.exp(b)
    qe = q * eb
    kdec = k * jnp.exp(b[C - 1:C, :] - b)
    eb_last = eb[C - 1:C, :]

    lane1 = lax.broadcasted_iota(jnp.int32, (1, LANES), 1)
    head_mask = (jnp.where(lane1 < B_KEY_DIM, 1.0, 0.0), jnp.where(lane1 >= B_KEY_DIM, 1.0, 0.0))
    lane_c = lax.broadcasted_iota(jnp.int32, (C, LANES), 1) % B_KEY_DIM
    row_c = lax.broadcasted_iota(jnp.int32, (C, LANES), 0)
    diag = jnp.where(lane_c == row_c, 1.0, 0.0)
    causal = jnp.where(lane_c <= row_c, 1.0, 0.0)
    same_head = jnp.where(lax.broadcasted_iota(jnp.int32, (LANES, LANES), 0) // B_KEY_DIM
                          == lax.broadcasted_iota(jnp.int32, (LANES, LANES), 1) // B_KEY_DIM, 1.0, 0.0).astype(BF16)
    tn = (((0,), (0,)), ((), ()))
    nt = (((1,), (1,)), ((), ()))

    for p in range(B_HEADS // 2):
        sl = slice(p * LANES, (p + 1) * LANES)
        kp = k[:, sl]
        bp = b[:, sl]

        def intra_rows(grp, carry):
            base = pl.multiple_of(grp * 8, 8)
            b8 = b_scr[pl.ds(base, 8), sl]
            q8 = q_scr[pl.ds(base, 8), sl]
            out = []
            for rr in range(8):
                if rr >= n_valid:
                    out.append(jnp.zeros((1, LANES), F32))
                    continue
                x = (q8[rr:rr + 1, :] * kp) * jnp.exp(jnp.minimum(b8[rr:rr + 1, :] - bp, 0.0))
                xh = x.astype(BF16)
                xl = (x - xh.astype(F32)).astype(BF16)
                seg = (jnp.dot(xh, same_head, preferred_element_type=F32)
                       + jnp.dot(xl, same_head, preferred_element_type=F32))
                out.append(jnp.sum(seg * diag, axis=0, keepdims=True))
            att_scr[pl.ds(base, 8), sl] = jnp.concatenate(out, axis=0)
            return carry

        lax.fori_loop(0, (n_valid + 7) // 8, intra_rows, 0)
        att = att_scr[:, sl] * causal
        sT = sT_ref[0, p]
        sT_bf = sT.astype(BF16)
        v_pair = jnp.concatenate([v[:, (2 * p) * B_VAL_DIM:(2 * p + 1) * B_VAL_DIM],
                                  v[:, (2 * p + 1) * B_VAL_DIM:(2 * p + 2) * B_VAL_DIM]], axis=0).astype(BF16)
        upd = jnp.zeros((B_VAL_DIM, LANES), F32)
        for u in range(2):
            h = 2 * p + u
            hs = slice(h * B_VAL_DIM, (h + 1) * B_VAL_DIM)
            o = lax.dot_general((qe[:, sl] * head_mask[u]).astype(BF16), sT_bf, nt, preferred_element_type=F32)
            o = o + jnp.dot((att * head_mask[u]).astype(BF16), v_pair, preferred_element_type=F32)
            o = o * lax.rsqrt(jnp.mean(o * o, axis=-1, keepdims=True) + EPS)
            rh = r_ref[0, :, hs]
            o_ref[0, :, hs] = (o * gn_ref[:, hs] * (rh * jax.nn.sigmoid(rh))).astype(o_ref.dtype)
            upd = upd + lax.dot_general(v[:, hs].astype(BF16), (kdec[:, sl] * head_mask[u]).astype(BF16), tn,
                                        preferred_element_type=F32)
        sT_ref[0, p] = sT * eb_last[:, sl] + upd


def _gla(qb, kb, vb, r, glow, w_gate_up, b_gate, gla_norm, s0, nb, length, n_valid):
    C = GLA_CHUNK
    assert length % C == 0 and B_KEY_DIM * 2 == LANES and C == B_KEY_DIM
    rs = lambda a: a.reshape(nb, length, a.shape[-1])
    wg = jnp.pad(w_gate_up, ((0, LANES - GATE_RANK), (0, 0)))
    s0T = s0.reshape(nb, 2, 2, B_KEY_DIM, B_VAL_DIM).transpose(0, 1, 4, 2, 3).reshape(nb, 2, B_VAL_DIM, LANES)
    tok = lambda w: pl.BlockSpec((1, C, w), lambda b, c: (b, c, 0))
    full = lambda shape: pl.BlockSpec(shape, lambda b, c: (0,) * len(shape))
    st = pl.BlockSpec((1, 2, B_VAL_DIM, LANES), lambda b, c: (b, 0, 0, 0))
    o, sT = pl.pallas_call(
        functools.partial(_gla_kernel, n_valid=n_valid),
        grid=(nb, length // C),
        in_specs=[tok(B_QK_WIDTH), tok(B_QK_WIDTH), tok(B_V_WIDTH), tok(B_V_WIDTH), tok(LANES),
                  full((LANES, B_QK_WIDTH)), full((1, B_QK_WIDTH)), full((1, B_V_WIDTH)), st],
        out_specs=[tok(B_V_WIDTH), st],
        out_shape=[jax.ShapeDtypeStruct((nb, length, B_V_WIDTH), BF16),
                   jax.ShapeDtypeStruct((nb, 2, B_VAL_DIM, LANES), F32)],
        scratch_shapes=[pltpu.VMEM((C, B_QK_WIDTH), F32), pltpu.VMEM((C, B_QK_WIDTH), F32),
                        pltpu.VMEM((C, B_QK_WIDTH), F32)],
        compiler_params=_cparams(("arbitrary", "arbitrary")),
        name="gla",
    )(rs(qb), rs(kb), rs(vb), rs(r), rs(glow), wg, b_gate.reshape(1, -1), gla_norm.reshape(1, -1), s0T)
    state = sT.reshape(nb, 2, B_VAL_DIM, 2, B_KEY_DIM).transpose(0, 1, 3, 4, 2).reshape(nb, B_HEADS, B_KEY_DIM, B_VAL_DIM)
    return o.reshape(nb * length, B_V_WIDTH), state


def _merge_kernel(oa_ref, ob_ref, ga_ref, gb_ref, x_ref, wa_ref, wb_ref, wo_ref, gf_ref, wr_ref, br_ref,
                  x2_ref, h2_ref, comb_ref, *, n_experts):
    ya = jnp.dot(oa_ref[...], wa_ref[...], preferred_element_type=F32)
    yb = jnp.dot(ob_ref[...], wb_ref[...], preferred_element_type=F32)
    merged = jax.nn.sigmoid(ga_ref[...]) * ya + jax.nn.sigmoid(gb_ref[...]) * yb
    x2 = x_ref[...] + jnp.dot(merged.astype(BF16), wo_ref[...], preferred_element_type=F32)
    x2_ref[...] = x2
    h2 = _rms(x2, gf_ref[...])
    h2_ref[...] = h2.astype(BF16)
    logits = jnp.dot(h2, wr_ref[...], precision=lax.Precision.HIGHEST, preferred_element_type=F32) + br_ref[...]
    lane = lax.broadcasted_iota(jnp.int32, logits.shape, 1).astype(F32)
    work = jnp.where(lane < n_experts, logits, -jnp.inf)
    picks = []
    for _ in range(TOP_K):
        m = jnp.max(work, axis=-1, keepdims=True)
        first = jnp.min(jnp.where(work == m, lane, float(LANES)), axis=-1, keepdims=True)
        hit = lane == first
        picks.append((m, hit))
        work = jnp.where(hit, -jnp.inf, work)
    es = [jnp.exp(m - picks[0][0]) for m, _ in picks]
    inv = 1.0 / (es[0] + es[1] + es[2] + es[3])
    comb = jnp.zeros_like(logits)
    for e, (_, hit) in zip(es, picks):
        comb = jnp.where(hit, e * inv, comb)
    comb_ref[...] = comb


def _merge(oa, ob, ga, gb, x2d, w_branch_a, w_branch_b, w_out, norm_ffn, w_router, b_router, tm):
    t, d = x2d.shape
    n_experts = w_router.shape[1]
    assert n_experts <= LANES and TOP_K == 4
    wr = jnp.pad(w_router, ((0, 0), (0, LANES - n_experts)))
    br = jnp.pad(b_router, (0, LANES - n_experts)).reshape(1, LANES)
    tok = lambda w: pl.BlockSpec((tm, w), lambda i: (i, 0))
    full = lambda a: pl.BlockSpec(a.shape, lambda i: (0,) * a.ndim)
    wa, wb, wo, gf = w_branch_a.astype(BF16), w_branch_b.astype(BF16), w_out.astype(BF16), norm_ffn.reshape(1, d)
    return pl.pallas_call(
        functools.partial(_merge_kernel, n_experts=n_experts),
        grid=(t // tm,),
        in_specs=[tok(A_WIDTH), tok(B_V_WIDTH), tok(d), tok(d), tok(d),
                  full(wa), full(wb), full(wo), full(gf), full(wr), full(br)],
        out_specs=[tok(d), tok(d), tok(LANES)],
        out_shape=[jax.ShapeDtypeStruct((t, d), F32), jax.ShapeDtypeStruct((t, d), BF16),
                   jax.ShapeDtypeStruct((t, LANES), F32)],
        compiler_params=_cparams(("arbitrary",)),
        name="merge",
    )(oa, ob, ga, gb, x2d, wa, wb, wo, gf, wr, br)


def _moe_kernel(h_ref, comb_ref, x2_ref, wug_ref, wul_ref, bug_ref, bul_ref, wd_ref, bd_ref, gfin_ref, o_ref):
    e = pl.program_id(1)
    half = pl.program_id(2)

    @pl.when((e == 0) & (half == 0))
    def _():
        o_ref[...] = jnp.zeros_like(o_ref)

    h = h_ref[...]
    glu = jnp.dot(h, wug_ref[0].astype(BF16), preferred_element_type=F32) + bug_ref[0]
    lin = jnp.dot(h, wul_ref[0].astype(BF16), preferred_element_type=F32) + bul_ref[0]
    glu = jnp.minimum(glu, SWIGLU_LIMIT)
    lin = jnp.clip(lin, -SWIGLU_LIMIT, SWIGLU_LIMIT)
    act = glu * jax.nn.sigmoid(SWIGLU_ALPHA * glu) * (lin + 1.0)
    y = jnp.dot(act.astype(BF16), wd_ref[0].astype(BF16), preferred_element_type=F32)
    y = y + jnp.where(half == 0, 1.0, 0.0) * bd_ref[0]
    lane = lax.broadcasted_iota(jnp.int32, comb_ref.shape, 1)
    c = jnp.sum(jnp.where(lane == e, comb_ref[...], 0.0), axis=-1, keepdims=True)
    o_ref[...] += c * y

    @pl.when((e == pl.num_programs(1) - 1) & (half == pl.num_programs(2) - 1))
    def _():
        o_ref[...] = _rms(x2_ref[...] + o_ref[...], gfin_ref[...])


def _moe(h2, comb, x2, w_up, b_up, w_down, b_down, norm_final, tm):
    t, d = x2.shape
    n_experts, _, two_de = w_up.shape
    de = two_de // 2
    nh = 2
    dh = de // nh
    b_up3 = b_up.reshape(n_experts, 1, two_de)
    b_down3 = b_down.reshape(n_experts, 1, d)
    tok = lambda w: pl.BlockSpec((tm, w), lambda i, e, f: (i, 0))
    return pl.pallas_call(
        _moe_kernel,
        grid=(t // tm, n_experts, nh),
        in_specs=[tok(d), tok(LANES), tok(d),
                  pl.BlockSpec((1, d, dh), lambda i, e, f: (e, 0, f)),
                  pl.BlockSpec((1, d, dh), lambda i, e, f: (e, 0, nh + f)),
                  pl.BlockSpec((1, 1, dh), lambda i, e, f: (e, 0, f)),
                  pl.BlockSpec((1, 1, dh), lambda i, e, f: (e, 0, nh + f)),
                  pl.BlockSpec((1, dh, d), lambda i, e, f: (e, f, 0)),
                  pl.BlockSpec((1, 1, d), lambda i, e, f: (e, 0, 0)),
                  pl.BlockSpec((1, d), lambda i, e, f: (0, 0))],
        out_specs=tok(d),
        out_shape=jax.ShapeDtypeStruct((t, d), F32),
        compiler_params=_cparams(("arbitrary", "arbitrary", "arbitrary")),
        name="moe",
    )(h2, comb, x2, w_up, w_up, b_up3, b_up3, w_down, b_down3, norm_final.reshape(1, d))


NEW_PAD = LANES
PAGES_PER_CHUNK = 8


def _sample_scores_kernel(pt_ref, qi_ref, w_ref, kin_ref, cki_ref, o_ref, buf, sem, *, n_pages, n_new):
    b = pl.program_id(0)
    slot = b % 2

    def page_copy(seq, p, sl):
        return pltpu.make_async_copy(cki_ref.at[pt_ref[seq, p]], buf.at[sl, p], sem.at[sl])

    @pl.when(b == 0)
    def _():
        for p in range(n_pages):
            page_copy(0, p, 0).start()

    @pl.when(b + 1 < pl.num_programs(0))
    def _():
        for p in range(n_pages):
            page_copy(b + 1, p, 1 - slot).start()

    for p in range(n_pages):
        page_copy(b, p, slot).wait()

    nt = (((1,), (1,)), ((), ()))
    qi = qi_ref[0]
    w = w_ref[0]
    half = IDX_HEADS * n_new // 2

    def reduce_heads(s, reps):
        wa = jnp.tile(w[0:half], (1, reps))
        wb = jnp.tile(w[half:2 * half], (1, reps))
        y = jnp.maximum(s[0:half], 0.0) * wa + jnp.maximum(s[half:2 * half], 0.0) * wb
        return (y + pltpu.roll(y, n_new, 0))[0:n_new]

    keys_per_chunk = PAGES_PER_CHUNK * PAGE_SIZE
    for c in range(n_pages // PAGES_PER_CHUNK):
        kc = buf[slot, c * PAGES_PER_CHUNK:(c + 1) * PAGES_PER_CHUNK].reshape(keys_per_chunk, IDX_DIM).astype(BF16)
        s = lax.dot_general(qi, kc, nt, preferred_element_type=F32)
        o_ref[0, :, c * keys_per_chunk:(c + 1) * keys_per_chunk] = reduce_heads(s, keys_per_chunk // LANES)
    s = lax.dot_general(qi, kin_ref[0], nt, preferred_element_type=F32)
    y = reduce_heads(s, 1)
    j = lax.broadcasted_iota(jnp.int32, (n_new, NEW_PAD), 1)
    t = lax.broadcasted_iota(jnp.int32, (n_new, NEW_PAD), 0)
    o_ref[0, :, n_pages * PAGE_SIZE:] = jnp.where(j <= t, y, -jnp.inf)


def _sample_scores(page_table, qi_rows, w_rows, ki_new_pad, cache_ki):
    nb, n_pages = page_table.shape
    n_new = qi_rows.shape[1] // IDX_HEADS
    assert n_new == 4 and n_pages % PAGES_PER_CHUNK == 0 and cache_ki.shape[1:] == (PAGE_SIZE, IDX_DIM)
    lk = n_pages * PAGE_SIZE + NEW_PAD
    grid_spec = pltpu.PrefetchScalarGridSpec(
        num_scalar_prefetch=1,
        grid=(nb,),
        in_specs=[pl.BlockSpec((1, IDX_HEADS * n_new, IDX_DIM), lambda b, pt: (b, 0, 0)),
                  pl.BlockSpec((1, IDX_HEADS * n_new, LANES), lambda b, pt: (b, 0, 0)),
                  pl.BlockSpec((1, NEW_PAD, IDX_DIM), lambda b, pt: (b, 0, 0)),
                  pl.BlockSpec(memory_space=pl.ANY)],
        out_specs=pl.BlockSpec((1, n_new, lk), lambda b, pt: (b, 0, 0)),
        scratch_shapes=[pltpu.VMEM((2, n_pages, PAGE_SIZE, IDX_DIM), F32), pltpu.SemaphoreType.DMA((2,))],
    )
    return pl.pallas_call(
        functools.partial(_sample_scores_kernel, n_pages=n_pages, n_new=n_new),
        grid_spec=grid_spec,
        out_shape=jax.ShapeDtypeStruct((nb, n_new, lk), F32),
        compiler_params=_cparams(("arbitrary",)),
        name="sample_scores",
    )(page_table, qi_rows, w_rows, ki_new_pad, cache_ki)


def _sample_select_kernel(s_ref, sel_ref, *, topk, n_past, n_new):
    rows, lk = s_ref.shape
    key = _sortable_key(s_ref[...])

    def count_ge(cand):
        return jnp.sum(jnp.where(key >= cand, 1.0, 0.0), axis=1, keepdims=True)

    def bisect(step, thr):
        cand = thr + lax.shift_left(jnp.int32(1), 31 - step)
        return jnp.where(count_ge(cand) >= topk, cand, thr)

    thr = lax.fori_loop(0, 32, bisect, jnp.full((rows, 1), INT_MIN, jnp.int32))
    n_take = topk - count_ge(thr + 1)
    col = lax.broadcasted_iota(jnp.int32, (rows, lk), 1)
    t = lax.broadcasted_iota(jnp.int32, (rows, lk), 0) % n_new
    visible = col - n_past <= t
    eq = jnp.where(key == thr, 1.0, 0.0)
    ra = lax.broadcasted_iota(jnp.int32, (LANES, LANES), 0)
    ca = lax.broadcasted_iota(jnp.int32, (LANES, LANES), 1)
    upper = jnp.where(ra < ca, 1.0, 0.0).astype(BF16)
    seen = jnp.zeros((rows, 1), F32)
    for c in range(lk // LANES):
        sl = slice(c * LANES, (c + 1) * LANES)
        eqc = eq[:, sl]
        before = seen + jnp.dot(eqc.astype(BF16), upper, preferred_element_type=F32)
        take = jnp.where(key[:, sl] > thr, 1.0, jnp.where(before < n_take, eqc, 0.0))
        sel_ref[:, sl] = jnp.where(visible[:, sl], take, 0.0)
        seen = seen + jnp.sum(eqc, axis=1, keepdims=True)


def _sample_select(scores2d, topk, n_past, n_new, rows):
    n, lk = scores2d.shape
    assert n % rows == 0 and rows % n_new == 0
    return pl.pallas_call(
        functools.partial(_sample_select_kernel, topk=topk, n_past=n_past, n_new=n_new),
        grid=(n // rows,),
        in_specs=[pl.BlockSpec((rows, lk), lambda i: (i, 0))],
        out_specs=pl.BlockSpec((rows, lk), lambda i: (i, 0)),
        out_shape=jax.ShapeDtypeStruct((n, lk), F32),
        compiler_params=_cparams(("arbitrary",)),
        name="sample_select",
    )(scores2d)


def _sample_attend_kernel(pt_ref, q_ref, sel_ref, bias_ref, kn_ref, vn_ref, ck_ref, cv_ref, o_ref,
                          kbuf, vbuf, ksem, vsem, *, n_pages, n_new):
    b = pl.program_id(0)
    n_chunks = n_pages // PAGES_PER_CHUNK
    keys_per_chunk = PAGES_PER_CHUNK * PAGE_SIZE
    rows = n_new * A_HEADS

    def chunk_copies(seq, c, sl):
        cps = []
        for p in range(PAGES_PER_CHUNK):
            page = pt_ref[seq, c * PAGES_PER_CHUNK + p]
            cps.append(pltpu.make_async_copy(ck_ref.at[page], kbuf.at[sl, p], ksem.at[sl]))
            cps.append(pltpu.make_async_copy(cv_ref.at[page], vbuf.at[sl, p], vsem.at[sl]))
        return cps

    @pl.when(b == 0)
    def _():
        for cp in chunk_copies(0, 0, 0):
            cp.start()

    hrow = lax.broadcasted_iota(jnp.int32, (A_HEADS, A_WIDTH), 0)
    hcol = lax.broadcasted_iota(jnp.int32, (A_HEADS, A_WIDTH), 1) // A_HEAD_DIM
    head_mask = jnp.where(hrow == hcol, 1.0, 0.0)
    qf = q_ref[0].astype(F32)
    qbd = jnp.concatenate([qf[t:t + 1, :] * head_mask for t in range(n_new)], axis=0).astype(BF16)
    nt = (((1,), (1,)), ((), ()))

    def softmax_step(state, logits, lanes, vmat):
        m_old, l_old, acc = state
        sel = sel_ref[0, :, lanes]
        sel_rows = jnp.concatenate([jnp.broadcast_to(sel[t:t + 1, :], (A_HEADS, sel.shape[1])) for t in range(n_new)],
                                   axis=0)
        x = jnp.where(sel_rows > 0.5, logits + bias_ref[:, lanes], NEG_BIG)
        m_new = jnp.maximum(m_old, jnp.max(x, axis=1, keepdims=True))
        alpha = jnp.exp(m_old - m_new)
        pm = jnp.exp(x - m_new)
        l_new = alpha * l_old + jnp.sum(pm, axis=1, keepdims=True)
        acc = alpha * acc + jnp.dot(pm.astype(BF16), vmat, preferred_element_type=F32)
        return m_new, l_new, acc

    def chunk_body(c, state):
        sl = c % 2

        @pl.when(c + 1 < n_chunks)
        def _():
            for cp in chunk_copies(b, c + 1, 1 - sl):
                cp.start()

        @pl.when((c + 1 == n_chunks) & (b + 1 < pl.num_programs(0)))
        def _():
            for cp in chunk_copies(b + 1, 0, 1 - sl):
                cp.start()

        for cp in chunk_copies(b, c, sl):
            cp.wait()
        kc = kbuf[sl].reshape(keys_per_chunk, A_WIDTH).astype(BF16)
        vc = vbuf[sl].reshape(keys_per_chunk, A_WIDTH).astype(BF16)
        logits = lax.dot_general(qbd, kc, nt, preferred_element_type=F32)
        lanes = pl.ds(pl.multiple_of(c * keys_per_chunk, keys_per_chunk), keys_per_chunk)
        return softmax_step(state, logits, lanes, vc)

    state = (jnp.full((rows, 1), NEG_BIG, F32), jnp.zeros((rows, 1), F32), jnp.zeros((rows, A_WIDTH), F32))
    state = lax.fori_loop(0, n_chunks, chunk_body, state)
    logits = lax.dot_general(qbd, kn_ref[0], nt, preferred_element_type=F32)
    _, l_fin, acc = softmax_step(state, logits, slice(n_pages * PAGE_SIZE, n_pages * PAGE_SIZE + NEW_PAD), vn_ref[0])
    full_mask = jnp.concatenate([head_mask] * n_new, axis=0)
    o = (acc / l_fin) * full_mask
    o_ref[0] = jnp.sum(o.reshape(n_new, A_HEADS, A_WIDTH), axis=1).astype(o_ref.dtype)


def _sample_attend(page_table, q_bf, sel, bias_rows, k_new_pad, v_new_pad, cache_k, cache_v):
    nb, n_pages = page_table.shape
    n_new = q_bf.shape[1]
    assert n_pages % (2 * PAGES_PER_CHUNK) == 0
    lk = n_pages * PAGE_SIZE + NEW_PAD
    grid_spec = pltpu.PrefetchScalarGridSpec(
        num_scalar_prefetch=1,
        grid=(nb,),
        in_specs=[pl.BlockSpec((1, n_new, A_WIDTH), lambda b, pt: (b, 0, 0)),
                  pl.BlockSpec((1, n_new, lk), lambda b, pt: (b, 0, 0)),
                  pl.BlockSpec((n_new * A_HEADS, lk), lambda b, pt: (0, 0)),
                  pl.BlockSpec((1, NEW_PAD, A_WIDTH), lambda b, pt: (b, 0, 0)),
                  pl.BlockSpec((1, NEW_PAD, A_WIDTH), lambda b, pt: (b, 0, 0)),
                  pl.BlockSpec(memory_space=pl.ANY),
                  pl.BlockSpec(memory_space=pl.ANY)],
        out_specs=pl.BlockSpec((1, n_new, A_WIDTH), lambda b, pt: (b, 0, 0)),
        scratch_shapes=[pltpu.VMEM((2, PAGES_PER_CHUNK, PAGE_SIZE, A_WIDTH), F32),
                        pltpu.VMEM((2, PAGES_PER_CHUNK, PAGE_SIZE, A_WIDTH), F32),
                        pltpu.SemaphoreType.DMA((2,)), pltpu.SemaphoreType.DMA((2,))],
    )
    return pl.pallas_call(
        functools.partial(_sample_attend_kernel, n_pages=n_pages, n_new=n_new),
        grid_spec=grid_spec,
        out_shape=jax.ShapeDtypeStruct((nb, n_new, A_WIDTH), BF16),
        compiler_params=_cparams(("arbitrary",)),
        name="sample_attend",
    )(page_table, q_bf, sel, bias_rows, k_new_pad, v_new_pad, cache_k, cache_v)


def _dsa_sample(qa_bf, kab, vab, qi_bf, kiw, rel_bias, cache_k, cache_v, cache_ki, page_table, n_new):
    nb, n_pages = page_table.shape
    n_past = n_pages * PAGE_SIZE
    topk = min(INDEX_TOPK, (n_past + n_new) // 4)
    qi_rows = qi_bf.reshape(nb, n_new, IDX_HEADS, IDX_DIM).transpose(0, 2, 1, 3).reshape(nb, IDX_HEADS * n_new, IDX_DIM)
    wi = kiw[:, IDX_DIM:IDX_DIM + IDX_HEADS] * INDEX_SCALE
    w_rows = jnp.broadcast_to(wi.reshape(nb, n_new, IDX_HEADS).transpose(0, 2, 1).reshape(nb, IDX_HEADS * n_new, 1),
                              (nb, IDX_HEADS * n_new, LANES))
    pad_new = lambda a: jnp.pad(a.reshape(nb, n_new, -1), ((0, 0), (0, NEW_PAD - n_new), (0, 0)))
    ki_new = pad_new(kiw[:, :IDX_DIM].astype(BF16))
    scores = _sample_scores(page_table, qi_rows, w_rows, ki_new, cache_ki)
    lk = scores.shape[-1]
    sel = _sample_select(scores.reshape(nb * n_new, lk), topk, n_past, n_new, rows=32)
    table = _bias_by_distance(rel_bias, REL_MAX_DIST + NEW_PAD)
    dist = (n_past + jnp.arange(n_new))[:, None] - jnp.arange(lk)[None, :]
    bias_rows = table[:, jnp.clip(dist, 0, table.shape[1] - 1)].transpose(1, 0, 2).reshape(n_new * A_HEADS, lk)
    o = _sample_attend(page_table, qa_bf.reshape(nb, n_new, A_WIDTH), sel.reshape(nb, n_new, lk), bias_rows,
                       pad_new(kab), pad_new(vab),
                       cache_k.reshape(cache_k.shape[0], PAGE_SIZE, A_WIDTH),
                       cache_v.reshape(cache_v.shape[0], PAGE_SIZE, A_WIDTH))
    return o.reshape(nb * n_new, A_WIDTH)


def kernel(x_prompt, x_sample, cache_k, cache_v, cache_idx_k, state_gla, page_table, norm_mix, w_in, w_gate_up, b_gate, gla_norm, w_branch_a, w_branch_b, w_out, norm_ffn, w_router, b_router, w_exp_up, b_exp_up, w_exp_down, b_exp_down, rel_bias, norm_final):
    batch, seq, d = x_prompt.shape
    nb, n_new, _ = x_sample.shape
    assert w_in.shape[0] == 1, "the final norm is fused into the single layer's MoE kernel"
    layer = 0
    w_packed = _pack_w_in(w_in[layer])

    def trunk(x2d, attend, gla_args, tm_moe):
        tokens = x2d.shape[0]
        tm = min(512, tokens)
        qa, ka, kab, va, vab, qi, kiw, qb, kb, vb, r, glow, ga, gb = _inproj(x2d, norm_mix[layer], w_packed, tm)
        oa = attend(qa, kab, vab, qi, kiw)
        ob, state = gla_args(qb, kb, vb, r, glow)
        x2, h2, comb = _merge(oa, ob, ga, gb, x2d, w_branch_a[layer], w_branch_b[layer], w_out[layer], norm_ffn[layer],
                              w_router[layer], b_router[layer], tm)
        y = _moe(h2, comb, x2, w_exp_up[layer], b_exp_up[layer], w_exp_down[layer], b_exp_down[layer], norm_final,
                 min(tm_moe, tokens))
        return y, ka, va, kiw[:, :IDX_DIM], state

    gla_w = (w_gate_up[layer], b_gate[layer], gla_norm[layer])

    def gla_prompt(qb, kb, vb, r, glow):
        s0 = jnp.zeros((batch, B_HEADS, B_KEY_DIM, B_VAL_DIM), F32)
        return _gla(qb, kb, vb, r, glow, *gla_w, s0, batch, seq, GLA_CHUNK)

    def gla_sample(qb, kb, vb, r, glow):
        pad = lambda a: jnp.pad(a.reshape(nb, n_new, -1), ((0, 0), (0, GLA_CHUNK - n_new), (0, 0))).reshape(nb * GLA_CHUNK, -1)
        o, state = _gla(pad(qb), pad(kb), pad(vb), pad(r), pad(glow), *gla_w, state_gla[layer].astype(F32), nb, GLA_CHUNK,
                        n_new)
        return o.reshape(nb, GLA_CHUNK, -1)[:, :n_new].reshape(nb * n_new, -1), state

    yp, kp, vp, kip, sp = trunk(
        x_prompt.reshape(batch * seq, d),
        lambda qa, kab, vab, qi, kiw: _dsa_prompt(qa, kab, vab, qi, kiw, rel_bias, batch, seq),
        gla_prompt, 1024)
    ys, ks, vs, kis, ss = trunk(
        x_sample.reshape(nb * n_new, d),
        lambda qa, kab, vab, qi, kiw: _dsa_sample(qa, kab, vab, qi, kiw, rel_bias, cache_k[layer], cache_v[layer],
                                                  cache_idx_k[layer], page_table, n_new),
        gla_sample, 512)
    heads = (A_HEADS, A_HEAD_DIM)
    return (yp.reshape(batch, seq, d), ys.reshape(nb, n_new, d),
            kp.reshape(1, batch, seq, *heads), vp.reshape(1, batch, seq, *heads), kip.reshape(1, batch, seq, IDX_DIM),
            sp.astype(x_prompt.dtype)[None],
            ks.reshape(1, nb, n_new, *heads), vs.reshape(1, nb, n_new, *heads), kis.reshape(1, nb, n_new, IDX_DIM),
            ss.astype(state_gla.dtype)[None])
```

```python
import functools
import math

import numpy as np
import jax
import jax.numpy as jnp
from jax import lax
from jax.experimental import pallas as pl
from jax.experimental.pallas import tpu as pltpu

F32 = jnp.float32
BF16 = jnp.bfloat16

A_HEADS = 8
A_HEAD_DIM = 64
A_WIDTH = A_HEADS * A_HEAD_DIM
IDX_HEADS = 4
IDX_DIM = 64
INDEX_TOPK = 256
INDEX_SCALE = (IDX_HEADS * IDX_DIM) ** -0.5
REL_BUCKETS = 32
REL_MAX_DIST = 128
B_HEADS = 4
B_KEY_DIM = 64
B_VAL_DIM = 128
B_QK_WIDTH = B_HEADS * B_KEY_DIM
B_V_WIDTH = B_HEADS * B_VAL_DIM
GATE_RANK = 16
GATE_TAU = 16.0
GLA_CHUNK = 64
TOP_K = 4
SWIGLU_ALPHA = 1.702
SWIGLU_LIMIT = 7.0
EPS = 1e-6
PAGE_SIZE = 128

LANES = 128
NEG_BIG = -1e30
INT_MIN = -2 ** 31
VMEM_LIMIT = 56 * 1024 * 1024


def _cparams(sem):
    return pltpu.CompilerParams(dimension_semantics=sem, vmem_limit_bytes=VMEM_LIMIT)


def _rms(xf, g):
    return xf * lax.rsqrt(jnp.mean(xf * xf, axis=-1, keepdims=True) + EPS) * g


_C_QA, _C_KA, _C_VA = 0, 512, 1024
_C_QI, _C_KIW = 1536, 1792
_C_QB, _C_KB, _C_VB, _C_R, _C_GLOW = 1920, 2176, 2432, 2944, 3456
_C_GA, _C_GB = 3584, 4608
_C_END = 5632


def _pack_w_in(w_in):
    d = w_in.shape[0]
    o = np.cumsum([0, 512, 512, 512, 256, 64, 4, 256, 256, 512, 16, 512, 1024, 1024]).tolist()
    seg = lambda i: w_in[:, o[i]:o[i + 1]]
    z = lambda n: jnp.zeros((d, n), w_in.dtype)
    packed = jnp.concatenate(
        [seg(0), seg(1), seg(2), seg(3), seg(4), seg(5), z(60),
         seg(6), seg(7), seg(8), seg(10), seg(9), z(112), seg(11), seg(12)], axis=1)
    assert packed.shape[1] == _C_END
    return packed.astype(BF16)


def _inproj_kernel(x_ref, g_ref, w_ref, qa_ref, ka_ref, kab_ref, va_ref, vab_ref, qi_ref, kiw_ref,
                   qb_ref, kb_ref, vb_ref, r_ref, glow_ref, ga_ref, gb_ref):
    h = _rms(x_ref[...], g_ref[...]).astype(BF16)

    def proj(a, b):
        return jnp.dot(h, w_ref[:, a:b], preferred_element_type=F32)

    za = proj(_C_QA, _C_QI)
    qa_ref[...] = (za[:, 0:512] * (A_HEAD_DIM ** -0.5)).astype(BF16)
    ka = za[:, 512:1024]
    va = za[:, 1024:1536]
    ka_ref[...] = ka
    kab_ref[...] = ka.astype(BF16)
    va_ref[...] = va
    vab_ref[...] = va.astype(BF16)
    zi = proj(_C_QI, _C_QB)
    qi_ref[...] = zi[:, 0:256].astype(BF16)
    kiw_ref[...] = zi[:, 256:384]
    zb = proj(_C_QB, _C_GA)
    qb_ref[...] = zb[:, 0:256]
    kb_ref[...] = zb[:, 256:512]
    vb_ref[...] = zb[:, 512:1024]
    r_ref[...] = zb[:, 1024:1536]
    glow_ref[...] = zb[:, 1536:1664]
    zg = proj(_C_GA, _C_END)
    ga_ref[...] = zg[:, 0:1024]
    gb_ref[...] = zg[:, 1024:2048]


def _inproj(x2d, g, w_packed, tm):
    t, d = x2d.shape
    widths = [(512, BF16), (512, F32), (512, BF16), (512, F32), (512, BF16), (256, BF16), (128, F32),
              (256, F32), (256, F32), (512, F32), (512, F32), (128, F32), (1024, F32), (1024, F32)]
    return pl.pallas_call(
        _inproj_kernel,
        grid=(t // tm,),
        in_specs=[pl.BlockSpec((tm, d), lambda i: (i, 0)),
                  pl.BlockSpec((1, d), lambda i: (0, 0)),
                  pl.BlockSpec((d, _C_END), lambda i: (0, 0))],
        out_specs=[pl.BlockSpec((tm, w), lambda i: (i, 0)) for w, _ in widths],
        out_shape=[jax.ShapeDtypeStruct((t, w), dt) for w, dt in widths],
        compiler_params=_cparams(("arbitrary",)),
        name="inproj",
    )(x2d, g.reshape(1, d), w_packed)


def _sortable_key(s):
    bits = pltpu.bitcast(s, jnp.int32)
    key = bits ^ ((bits >> 31) & 0x7FFFFFFF)
    return jnp.where(key == -1, 0, key)


def _t5_bucket_np(dist):
    max_exact = REL_BUCKETS // 2
    ratio = np.log(np.maximum(dist, 1).astype(np.float32) / np.float32(max_exact)) / np.float32(
        math.log(REL_MAX_DIST / max_exact))
    large = np.minimum(max_exact + (ratio * (REL_BUCKETS - max_exact)).astype(np.int32), REL_BUCKETS - 1)
    return np.where(dist < max_exact, dist, large).astype(np.int32)


def _bias_by_distance(rel_bias, n):
    dist = jnp.arange(n, dtype=jnp.int32)
    max_exact = REL_BUCKETS // 2
    log_ratio = jnp.log(jnp.maximum(dist, 1).astype(F32) / max_exact) / math.log(REL_MAX_DIST / max_exact)
    large = jnp.minimum(max_exact + (log_ratio * (REL_BUCKETS - max_exact)).astype(jnp.int32), REL_BUCKETS - 1)
    bucket = jnp.where(dist < max_exact, dist, large)
    return rel_bias[bucket].T.astype(F32)


def _dsa_prompt_kernel(rel_ref, qT_ref, qiT_ref, wT_ref, k_ref, vT_ref, ki_ref, bkt_ref, o_ref,
                       s_scr, qbd_scr, acc_scr, m_scr, l_scr, bias_scr, mask_scr, *, tq, tk, topk):
    i = pl.program_id(1)

    @pl.when((pl.program_id(0) == 0) & (i == 0))
    def _():
        for which in range(2 * (tk // tq)):
            bkt = bkt_ref[which]

            def one_head(h, carry):
                far = rel_ref[REL_BUCKETS - 1, h]
                bias_scr[which, h] = lax.fori_loop(
                    0, REL_BUCKETS - 1, lambda b, t: jnp.where(bkt == b, rel_ref[b, h] - far, t),
                    jnp.zeros((tk, tq), F32))
                return carry

            lax.fori_loop(0, A_HEADS, one_head, 0)

    diag = (i * tq) // tk
    nkb = diag + 1
    qpos = i * tq + lax.broadcasted_iota(jnp.int32, (1, tq), 1)
    krow = lax.broadcasted_iota(jnp.int32, (tk, 1), 0)

    qiT = qiT_ref[0]
    qi_st = jnp.concatenate([qiT[h * IDX_DIM:(h + 1) * IDX_DIM, :] for h in range(IDX_HEADS)], axis=1)
    w = wT_ref[0]

    def score_block(j, carry):
        s = jnp.dot(ki_ref[0, j], qi_st, preferred_element_type=F32)
        acc = jnp.maximum(s[:, 0:tq], 0.0) * w[0:1, :]
        for h in range(1, IDX_HEADS):
            acc = acc + jnp.maximum(s[:, h * tq:(h + 1) * tq], 0.0) * w[h:h + 1, :]
        acc = jnp.where(j * tk + krow <= qpos, acc, -jnp.inf)
        s_scr[pl.ds(pl.multiple_of(j * tk, tk), tk), :] = _sortable_key(acc)
        return carry

    lax.fori_loop(0, nkb, score_block, 0)

    def count_ge(cand):
        def body(j, c):
            blk = s_scr[pl.ds(pl.multiple_of(j * tk, tk), tk), :]
            hit = jnp.where(blk >= cand, 1, 0).astype(jnp.int32)
            return c + jnp.sum(hit.reshape(tk // 8, 8, tq), axis=0)
        c = lax.fori_loop(0, nkb, body, jnp.zeros((8, tq), jnp.int32))
        return jnp.sum(c, axis=0, keepdims=True)

    def bisect(step, thr):
        cand = thr + lax.shift_left(jnp.int32(1), 31 - step)
        return jnp.where(count_ge(cand) >= topk, cand, thr)

    thr = lax.fori_loop(0, 32, bisect, jnp.full((1, tq), INT_MIN, jnp.int32))
    n_take = (topk - count_ge(thr + 1)).astype(F32)

    rowid = lax.broadcasted_iota(jnp.int32, (2 * A_HEAD_DIM, tq), 0)
    for p in range(A_HEADS // 2):
        slab = qT_ref[0, p * 128:(p + 1) * 128, :].astype(F32)
        top = jnp.where(rowid < A_HEAD_DIM, slab, 0.0)
        qbd_scr[p] = jnp.concatenate([top, slab - top], axis=1).astype(BF16)

    acc_scr[...] = jnp.zeros_like(acc_scr)
    m_scr[...] = jnp.full_like(m_scr, NEG_BIG)
    l_scr[...] = jnp.zeros_like(l_scr)
    ri = lax.broadcasted_iota(jnp.int32, (tk, tk), 0)
    ci = lax.broadcasted_iota(jnp.int32, (tk, tk), 1)
    ltri = jnp.where(ci < ri, 1.0, 0.0).astype(BF16)

    def attend_block(j, tie_seen, near):
        keyblk = s_scr[pl.ds(pl.multiple_of(j * tk, tk), tk), :]
        eq = jnp.where(keyblk == thr, 1.0, 0.0)
        before = tie_seen + jnp.dot(ltri, eq.astype(BF16), preferred_element_type=F32)
        take = jnp.where(keyblk > thr, 1.0, jnp.where(before < n_take, eq, 0.0))
        if near:
            take = jnp.where(j * tk + krow <= qpos, take, 0.0)
            which = 2 * ((i * tq) % tk // tq) + (diag - j)
        mask_scr[...] = jnp.where(take > 0.5, 0.0, NEG_BIG)
        for p in range(A_HEADS // 2):
            lg = jnp.dot(k_ref[0, j, :, p * 128:(p + 1) * 128], qbd_scr[p], preferred_element_type=F32)
            for u in range(2):
                h = 2 * p + u
                x = lg[:, u * tq:(u + 1) * tq] + mask_scr[...]
                if near:
                    x = x + bias_scr[which, h]
                m_old = m_scr[h:h + 1, :]
                m_new = jnp.maximum(m_old, jnp.max(x, axis=0, keepdims=True))
                alpha = jnp.exp(m_old - m_new)
                pm = jnp.exp(x - m_new)
                l_scr[h:h + 1, :] = alpha * l_scr[h:h + 1, :] + jnp.sum(pm, axis=0, keepdims=True)
                pv = jnp.dot(vT_ref[0, j, h * A_HEAD_DIM:(h + 1) * A_HEAD_DIM, :], pm.astype(BF16),
                             preferred_element_type=F32)
                rows = slice(h * A_HEAD_DIM, (h + 1) * A_HEAD_DIM)
                acc_scr[rows, :] = alpha * acc_scr[rows, :] + pv
                m_scr[h:h + 1, :] = m_new
        return tie_seen + jnp.sum(eq, axis=0, keepdims=True)

    n_far = jnp.maximum(diag - 1, 0)
    tie_seen = lax.fori_loop(0, n_far, functools.partial(attend_block, near=False), jnp.zeros((1, tq), F32))
    lax.fori_loop(n_far, nkb, functools.partial(attend_block, near=True), tie_seen)

    for h in range(A_HEADS):
        rows = slice(h * A_HEAD_DIM, (h + 1) * A_HEAD_DIM)
        acc_scr[rows, :] = acc_scr[rows, :] / l_scr[h:h + 1, :]
    o_ref[0] = acc_scr[...].T.astype(o_ref.dtype)


def _dsa_prompt(qa_bf, ka_bf, va_bf, qi_bf, kiw, rel_bias, batch, seq):
    tq, tk = 256, 256
    assert seq % tk == 0 and tk % tq == 0
    topk = min(INDEX_TOPK, seq // 4)
    nk = seq // tk
    qT = qa_bf.reshape(batch, seq, A_WIDTH).transpose(0, 2, 1)
    qiT = qi_bf.reshape(batch, seq, IDX_HEADS * IDX_DIM).transpose(0, 2, 1)
    wi = kiw[:, IDX_DIM:IDX_DIM + IDX_HEADS] * INDEX_SCALE
    wT = jnp.pad(wi.reshape(batch, seq, IDX_HEADS).transpose(0, 2, 1), ((0, 0), (0, 8 - IDX_HEADS), (0, 0)))
    k4 = ka_bf.reshape(batch, nk, tk, A_WIDTH)
    vT4 = va_bf.reshape(batch, nk, tk, A_WIDTH).transpose(0, 1, 3, 2)
    ki4 = kiw[:, :IDX_DIM].astype(BF16).reshape(batch, nk, tk, IDX_DIM)
    assert REL_MAX_DIST <= tk
    d0 = np.arange(tq)[None, :] - np.arange(tk)[:, None]
    bkt = jnp.asarray(np.stack([_t5_bucket_np(np.maximum(d0 + off * tq + kind * tk, 0))
                                for off in range(tk // tq) for kind in range(2)]), jnp.int32)

    once = dict(pipeline_mode=pl.Buffered(1))
    kern = functools.partial(_dsa_prompt_kernel, tq=tq, tk=tk, topk=topk)
    out = pl.pallas_call(
        kern,
        grid=(batch, seq // tq),
        in_specs=[pl.BlockSpec(memory_space=pltpu.SMEM),
                  pl.BlockSpec((1, A_WIDTH, tq), lambda b, i: (b, 0, i)),
                  pl.BlockSpec((1, IDX_HEADS * IDX_DIM, tq), lambda b, i: (b, 0, i)),
                  pl.BlockSpec((1, 8, tq), lambda b, i: (b, 0, i)),
                  pl.BlockSpec((1, nk, tk, A_WIDTH), lambda b, i: (b, 0, 0, 0), **once),
                  pl.BlockSpec((1, nk, A_WIDTH, tk), lambda b, i: (b, 0, 0, 0), **once),
                  pl.BlockSpec((1, nk, tk, IDX_DIM), lambda b, i: (b, 0, 0, 0), **once),
                  pl.BlockSpec(bkt.shape, lambda b, i: (0, 0, 0), **once)],
        out_specs=pl.BlockSpec((1, tq, A_WIDTH), lambda b, i: (b, i, 0)),
        out_shape=jax.ShapeDtypeStruct((batch, seq, A_WIDTH), BF16),
        scratch_shapes=[pltpu.VMEM((seq, tq), jnp.int32),
                        pltpu.VMEM((A_HEADS // 2, 2 * A_HEAD_DIM, 2 * tq), BF16),
                        pltpu.VMEM((A_WIDTH, tq), F32),
                        pltpu.VMEM((A_HEADS, tq), F32),
                        pltpu.VMEM((A_HEADS, tq), F32),
                        pltpu.VMEM((bkt.shape[0], A_HEADS, tk, tq), F32),
                        pltpu.VMEM((tk, tq), F32)],
        compiler_params=_cparams(("arbitrary", "arbitrary")),
        name="dsa_prompt",
    )(rel_bias.astype(F32), qT, qiT, wT, k4, vT4, ki4, bkt)
    return out.reshape(batch * seq, A_WIDTH)


def _gla_kernel(q_ref, k_ref, v_ref, r_ref, glow_ref, wg_ref, bg_ref, gn_ref, s0_ref, o_ref, s_out_ref,
                sT_ref, q_scr, b_scr, att_scr, *, n_valid):
    C = GLA_CHUNK
    hp = lax.Precision.HIGHEST

    @pl.when(pl.program_id(1) == 0)
    def _():
        for p in range(B_HEADS // 2):
            sT_ref[p] = s0_ref[0, 2 * p:2 * p + 2].reshape(LANES, B_VAL_DIM).T

    g = jnp.dot(glow_ref[0], wg_ref[...], precision=hp, preferred_element_type=F32) + bg_ref[...]
    log_a = (jnp.minimum(g, 0.0) - jnp.log1p(jnp.exp(-jnp.abs(g)))) * (1.0 / GATE_TAU)
    if n_valid < C:
        log_a = jnp.where(lax.broadcasted_iota(jnp.int32, (C, 1), 0) < n_valid, log_a, 0.0)
        att_scr[...] = jnp.zeros_like(att_scr)
    tri = jnp.where(lax.broadcasted_iota(jnp.int32, (C, C), 1) <= lax.broadcasted_iota(jnp.int32, (C, C), 0), 1.0, 0.0)
    b = jnp.dot(tri, log_a, precision=hp, preferred_element_type=F32)
    q = q_ref[0] * (B_KEY_DIM ** -0.5)
    k = k_ref[0]
    v = v_ref[0]
    q_scr[...] = q
    b_scr[...] = b
    eb = jnp.exp(b)
    qe = q * eb
    kdec = k * jnp.exp(b[C - 1:C, :] - b)
    eb_last = eb[C - 1:C, :]

    lane1 = lax.broadcasted_iota(jnp.int32, (1, LANES), 1)
    head_mask = (jnp.where(lane1 < B_KEY_DIM, 1.0, 0.0), jnp.where(lane1 >= B_KEY_DIM, 1.0, 0.0))
    lane_c = lax.broadcasted_iota(jnp.int32, (C, LANES), 1) % B_KEY_DIM
    row_c = lax.broadcasted_iota(jnp.int32, (C, LANES), 0)
    diag = jnp.where(lane_c == row_c, 1.0, 0.0)
    causal = jnp.where(lane_c <= row_c, 1.0, 0.0)
    same_head = jnp.where(lax.broadcasted_iota(jnp.int32, (LANES, LANES), 0) // B_KEY_DIM
                          == lax.broadcasted_iota(jnp.int32, (LANES, LANES), 1) // B_KEY_DIM, 1.0, 0.0).astype(BF16)
    tn = (((0,), (0,)), ((), ()))
    nt = (((1,), (1,)), ((), ()))

    for p in range(B_HEADS // 2):
        sl = slice(p * LANES, (p + 1) * LANES)
        kp = k[:, sl]
        bp = b[:, sl]

        def intra_rows(grp, carry):
            base = pl.multiple_of(grp * 8, 8)
            b8 = b_scr[pl.ds(base, 8), sl]
            q8 = q_scr[pl.ds(base, 8), sl]
            out = []
            for rr in range(8):
                if rr >= n_valid:
                    out.append(jnp.zeros((1, LANES), F32))
                    continue
                x = (q8[rr:rr + 1, :] * kp) * jnp.exp(jnp.minimum(b8[rr:rr + 1, :] - bp, 0.0))
                xh = x.astype(BF16)
                xl = (x - xh.astype(F32)).astype(BF16)
                seg = (jnp.dot(xh, same_head, preferred_element_type=F32)
                       + jnp.dot(xl, same_head, preferred_element_type=F32))
                out.append(jnp.sum(seg * diag, axis=0, keepdims=True))
            att_scr[pl.ds(base, 8), sl] = jnp.concatenate(out, axis=0)
            return carry

        lax.fori_loop(0, (n_valid + 7) // 8, intra_rows, 0)
        att = att_scr[:, sl] * causal
        sT = sT_ref[p]
        sT_bf = sT.astype(BF16)
        v_pair = jnp.concatenate([v[:, (2 * p) * B_VAL_DIM:(2 * p + 1) * B_VAL_DIM],
                                  v[:, (2 * p + 1) * B_VAL_DIM:(2 * p + 2) * B_VAL_DIM]], axis=0).astype(BF16)
        upd = jnp.zeros((B_VAL_DIM, LANES), F32)
        for u in range(2):
            h = 2 * p + u
            hs = slice(h * B_VAL_DIM, (h + 1) * B_VAL_DIM)
            o = lax.dot_general((qe[:, sl] * head_mask[u]).astype(BF16), sT_bf, nt, preferred_element_type=F32)
            o = o + jnp.dot((att * head_mask[u]).astype(BF16), v_pair, preferred_element_type=F32)
            o = o * lax.rsqrt(jnp.mean(o * o, axis=-1, keepdims=True) + EPS)
            rh = r_ref[0, :, hs]
            o_ref[0, :, hs] = (o * gn_ref[:, hs] * (rh * jax.nn.sigmoid(rh))).astype(o_ref.dtype)
            upd = upd + lax.dot_general(v[:, hs].astype(BF16), (kdec[:, sl] * head_mask[u]).astype(BF16), tn,
                                        preferred_element_type=F32)
        s_new = sT * eb_last[:, sl] + upd
        sT_ref[p] = s_new

        @pl.when(pl.program_id(1) == pl.num_programs(1) - 1)
        def _():
            s_out_ref[0, 2 * p:2 * p + 2] = s_new.T.reshape(2, B_KEY_DIM, B_VAL_DIM)


def _gla(qb, kb, vb, r, glow, w_gate_up, b_gate, gla_norm, s0, nb, length, n_valid):
    C = GLA_CHUNK
    assert length % C == 0 and B_KEY_DIM * 2 == LANES and C == B_KEY_DIM and B_VAL_DIM == LANES
    rs = lambda a: a.reshape(nb, length, a.shape[-1])
    wg = jnp.pad(w_gate_up, ((0, LANES - GATE_RANK), (0, 0)))
    tok = lambda w: pl.BlockSpec((1, C, w), lambda b, c: (b, c, 0))
    full = lambda shape: pl.BlockSpec(shape, lambda b, c: (0,) * len(shape))
    st = pl.BlockSpec((1, B_HEADS, B_KEY_DIM, B_VAL_DIM), lambda b, c: (b, 0, 0, 0))
    o, state = pl.pallas_call(
        functools.partial(_gla_kernel, n_valid=n_valid),
        grid=(nb, length // C),
        in_specs=[tok(B_QK_WIDTH), tok(B_QK_WIDTH), tok(B_V_WIDTH), tok(B_V_WIDTH), tok(LANES),
                  full((LANES, B_QK_WIDTH)), full((1, B_QK_WIDTH)), full((1, B_V_WIDTH)), st],
        out_specs=[tok(B_V_WIDTH), st],
        out_shape=[jax.ShapeDtypeStruct((nb, length, B_V_WIDTH), BF16),
                   jax.ShapeDtypeStruct((nb, B_HEADS, B_KEY_DIM, B_VAL_DIM), F32)],
        scratch_shapes=[pltpu.VMEM((B_HEADS // 2, B_VAL_DIM, LANES), F32), pltpu.VMEM((C, B_QK_WIDTH), F32),
                        pltpu.VMEM((C, B_QK_WIDTH), F32), pltpu.VMEM((C, B_QK_WIDTH), F32)],
        compiler_params=_cparams(("arbitrary", "arbitrary")),
        name="gla",
    )(rs(qb), rs(kb), rs(vb), rs(r), rs(glow), wg, b_gate.reshape(1, -1), gla_norm.reshape(1, -1), s0)
    return o.reshape(nb * length, B_V_WIDTH), state


def _merge_kernel(oa_ref, ob_ref, ga_ref, gb_ref, x_ref, wa_ref, wb_ref, wo_ref, gf_ref, wr_ref, br_ref,
                  x2_ref, h2_ref, comb_ref, *, n_experts):
    ya = jnp.dot(oa_ref[...], wa_ref[...], preferred_element_type=F32)
    yb = jnp.dot(ob_ref[...], wb_ref[...], preferred_element_type=F32)
    merged = jax.nn.sigmoid(ga_ref[...]) * ya + jax.nn.sigmoid(gb_ref[...]) * yb
    x2 = x_ref[...] + jnp.dot(merged.astype(BF16), wo_ref[...], preferred_element_type=F32)
    x2_ref[...] = x2
    h2 = _rms(x2, gf_ref[...])
    h2_ref[...] = h2.astype(BF16)
    logits = jnp.dot(h2, wr_ref[...], precision=lax.Precision.HIGHEST, preferred_element_type=F32) + br_ref[...]
    lane = lax.broadcasted_iota(jnp.int32, logits.shape, 1).astype(F32)
    work = jnp.where(lane < n_experts, logits, -jnp.inf)
    picks = []
    for _ in range(TOP_K):
        m = jnp.max(work, axis=-1, keepdims=True)
        first = jnp.min(jnp.where(work == m, lane, float(LANES)), axis=-1, keepdims=True)
        hit = lane == first
        picks.append((m, hit))
        work = jnp.where(hit, -jnp.inf, work)
    es = [jnp.exp(m - picks[0][0]) for m, _ in picks]
    inv = 1.0 / (es[0] + es[1] + es[2] + es[3])
    comb = jnp.zeros_like(logits)
    for e, (_, hit) in zip(es, picks):
        comb = jnp.where(hit, e * inv, comb)
    comb_ref[...] = comb


def _merge(oa, ob, ga, gb, x2d, w_branch_a, w_branch_b, w_out, norm_ffn, w_router, b_router, tm):
    t, d = x2d.shape
    n_experts = w_router.shape[1]
    assert n_experts <= LANES and TOP_K == 4
    wr = jnp.pad(w_router, ((0, 0), (0, LANES - n_experts)))
    br = jnp.pad(b_router, (0, LANES - n_experts)).reshape(1, LANES)
    tok = lambda w: pl.BlockSpec((tm, w), lambda i: (i, 0))
    full = lambda a: pl.BlockSpec(a.shape, lambda i: (0,) * a.ndim)
    wa, wb, wo, gf = w_branch_a.astype(BF16), w_branch_b.astype(BF16), w_out.astype(BF16), norm_ffn.reshape(1, d)
    return pl.pallas_call(
        functools.partial(_merge_kernel, n_experts=n_experts),
        grid=(t // tm,),
        in_specs=[tok(A_WIDTH), tok(B_V_WIDTH), tok(d), tok(d), tok(d),
                  full(wa), full(wb), full(wo), full(gf), full(wr), full(br)],
        out_specs=[tok(d), tok(d), tok(LANES)],
        out_shape=[jax.ShapeDtypeStruct((t, d), F32), jax.ShapeDtypeStruct((t, d), BF16),
                   jax.ShapeDtypeStruct((t, LANES), F32)],
        compiler_params=_cparams(("arbitrary",)),
        name="merge",
    )(oa, ob, ga, gb, x2d, wa, wb, wo, gf, wr, br)


def _moe_kernel(h_ref, comb_ref, x2_ref, wug_ref, wul_ref, bug_ref, bul_ref, wd_ref, bd_ref, gfin_ref, o_ref):
    e = pl.program_id(1)
    half = pl.program_id(2)

    @pl.when((e == 0) & (half == 0))
    def _():
        o_ref[...] = jnp.zeros_like(o_ref)

    h = h_ref[...]
    glu = jnp.dot(h, wug_ref[0].astype(BF16), preferred_element_type=F32) + bug_ref[0]
    lin = jnp.dot(h, wul_ref[0].astype(BF16), preferred_element_type=F32) + bul_ref[0]
    glu = jnp.minimum(glu, SWIGLU_LIMIT)
    lin = jnp.clip(lin, -SWIGLU_LIMIT, SWIGLU_LIMIT)
    act = glu * jax.nn.sigmoid(SWIGLU_ALPHA * glu) * (lin + 1.0)
    y = jnp.dot(act.astype(BF16), wd_ref[0].astype(BF16), preferred_element_type=F32)
    y = y + jnp.where(half == 0, 1.0, 0.0) * bd_ref[0]
    lane = lax.broadcasted_iota(jnp.int32, comb_ref.shape, 1)
    c = jnp.sum(jnp.where(lane == e, comb_ref[...], 0.0), axis=-1, keepdims=True)
    o_ref[...] += c * y

    @pl.when((e == pl.num_programs(1) - 1) & (half == pl.num_programs(2) - 1))
    def _():
        o_ref[...] = _rms(x2_ref[...] + o_ref[...], gfin_ref[...])


def _moe(h2, comb, x2, w_up, b_up, w_down, b_down, norm_final, tm):
    t, d = x2.shape
    n_experts, _, two_de = w_up.shape
    de = two_de // 2
    nh = 2
    dh = de // nh
    b_up3 = b_up.reshape(n_experts, 1, two_de)
    b_down3 = b_down.reshape(n_experts, 1, d)
    tok = lambda w: pl.BlockSpec((tm, w), lambda i, e, f: (i, 0))
    return pl.pallas_call(
        _moe_kernel,
        grid=(t // tm, n_experts, nh),
        in_specs=[tok(d), tok(LANES), tok(d),
                  pl.BlockSpec((1, d, dh), lambda i, e, f: (e, 0, f)),
                  pl.BlockSpec((1, d, dh), lambda i, e, f: (e, 0, nh + f)),
                  pl.BlockSpec((1, 1, dh), lambda i, e, f: (e, 0, f)),
                  pl.BlockSpec((1, 1, dh), lambda i, e, f: (e, 0, nh + f)),
                  pl.BlockSpec((1, dh, d), lambda i, e, f: (e, f, 0)),
                  pl.BlockSpec((1, 1, d), lambda i, e, f: (e, 0, 0)),
                  pl.BlockSpec((1, d), lambda i, e, f: (0, 0))],
        out_specs=tok(d),
        out_shape=jax.ShapeDtypeStruct((t, d), F32),
        compiler_params=_cparams(("arbitrary", "arbitrary", "arbitrary")),
        name="moe",
    )(h2, comb, x2, w_up, w_up, b_up3, b_up3, w_down, b_down3, norm_final.reshape(1, d))


NEW_PAD = LANES
PAGES_PER_CHUNK = 8


def _sample_scores_kernel(pt_ref, qi_ref, w_ref, kin_ref, cki_ref, o_ref, buf, sem, *, n_pages, n_new):
    b = pl.program_id(0)
    slot = b % 2

    def page_copy(seq, p, sl):
        return pltpu.make_async_copy(cki_ref.at[pt_ref[seq, p]], buf.at[sl, p], sem.at[sl])

    @pl.when(b == 0)
    def _():
        for p in range(n_pages):
            page_copy(0, p, 0).start()

    @pl.when(b + 1 < pl.num_programs(0))
    def _():
        for p in range(n_pages):
            page_copy(b + 1, p, 1 - slot).start()

    for p in range(n_pages):
        page_copy(b, p, slot).wait()

    nt = (((1,), (1,)), ((), ()))
    qi = qi_ref[0]
    w = w_ref[0]
    half = IDX_HEADS * n_new // 2

    def reduce_heads(s, reps):
        wa = jnp.tile(w[0:half], (1, reps))
        wb = jnp.tile(w[half:2 * half], (1, reps))
        y = jnp.maximum(s[0:half], 0.0) * wa + jnp.maximum(s[half:2 * half], 0.0) * wb
        return (y + pltpu.roll(y, n_new, 0))[0:n_new]

    keys_per_chunk = PAGES_PER_CHUNK * PAGE_SIZE
    for c in range(n_pages // PAGES_PER_CHUNK):
        kc = buf[slot, c * PAGES_PER_CHUNK:(c + 1) * PAGES_PER_CHUNK].reshape(keys_per_chunk, IDX_DIM).astype(BF16)
        s = lax.dot_general(qi, kc, nt, preferred_element_type=F32)
        o_ref[0, :, c * keys_per_chunk:(c + 1) * keys_per_chunk] = reduce_heads(s, keys_per_chunk // LANES)
    s = lax.dot_general(qi, kin_ref[0], nt, preferred_element_type=F32)
    y = reduce_heads(s, 1)
    j = lax.broadcasted_iota(jnp.int32, (n_new, NEW_PAD), 1)
    t = lax.broadcasted_iota(jnp.int32, (n_new, NEW_PAD), 0)
    o_ref[0, :, n_pages * PAGE_SIZE:] = jnp.where(j <= t, y, -jnp.inf)


def _sample_scores(page_table, qi_rows, w_rows, ki_new_pad, cache_ki):
    nb, n_pages = page_table.shape
    n_new = qi_rows.shape[1] // IDX_HEADS
    assert n_new == 4 and n_pages % PAGES_PER_CHUNK == 0 and cache_ki.shape[1:] == (PAGE_SIZE, IDX_DIM)
    lk = n_pages * PAGE_SIZE + NEW_PAD
    grid_spec = pltpu.PrefetchScalarGridSpec(
        num_scalar_prefetch=1,
        grid=(nb,),
        in_specs=[pl.BlockSpec((1, IDX_HEADS * n_new, IDX_DIM), lambda b, pt: (b, 0, 0)),
                  pl.BlockSpec((1, IDX_HEADS * n_new, LANES), lambda b, pt: (b, 0, 0)),
                  pl.BlockSpec((1, NEW_PAD, IDX_DIM), lambda b, pt: (b, 0, 0)),
                  pl.BlockSpec(memory_space=pl.ANY)],
        out_specs=pl.BlockSpec((1, n_new, lk), lambda b, pt: (b, 0, 0)),
        scratch_shapes=[pltpu.VMEM((2, n_pages, PAGE_SIZE, IDX_DIM), F32), pltpu.SemaphoreType.DMA((2,))],
    )
    return pl.pallas_call(
        functools.partial(_sample_scores_kernel, n_pages=n_pages, n_new=n_new),
        grid_spec=grid_spec,
        out_shape=jax.ShapeDtypeStruct((nb, n_new, lk), F32),
        compiler_params=_cparams(("arbitrary",)),
        name="sample_scores",
    )(page_table, qi_rows, w_rows, ki_new_pad, cache_ki)


def _sample_select_kernel(s_ref, sel_ref, *, topk, n_past, n_new):
    rows, lk = s_ref.shape
    key = _sortable_key(s_ref[...])

    def count_ge(cand):
        return jnp.sum(jnp.where(key >= cand, 1.0, 0.0), axis=1, keepdims=True)

    def bisect(step, thr):
        cand = thr + lax.shift_left(jnp.int32(1), 31 - step)
        return jnp.where(count_ge(cand) >= topk, cand, thr)

    thr = lax.fori_loop(0, 32, bisect, jnp.full((rows, 1), INT_MIN, jnp.int32))
    n_take = topk - count_ge(thr + 1)
    col = lax.broadcasted_iota(jnp.int32, (rows, lk), 1)
    t = lax.broadcasted_iota(jnp.int32, (rows, lk), 0) % n_new
    visible = col - n_past <= t
    eq = jnp.where(key == thr, 1.0, 0.0)
    ra = lax.broadcasted_iota(jnp.int32, (LANES, LANES), 0)
    ca = lax.broadcasted_iota(jnp.int32, (LANES, LANES), 1)
    upper = jnp.where(ra < ca, 1.0, 0.0).astype(BF16)
    wide = LANES * A_HEADS
    spread = jnp.where(lax.broadcasted_iota(jnp.int32, (LANES, wide), 1) // A_HEADS
                       == lax.broadcasted_iota(jnp.int32, (LANES, wide), 0), 1.0, 0.0).astype(BF16)
    seen = jnp.zeros((rows, 1), F32)
    n_blocks = lk // LANES
    for c in range(n_blocks):
        sl = slice(c * LANES, (c + 1) * LANES)
        eqc = eq[:, sl]
        before = seen + jnp.dot(eqc.astype(BF16), upper, preferred_element_type=F32)
        take = jnp.where(key[:, sl] > thr, 1.0, jnp.where(before < n_take, eqc, 0.0))
        take = jnp.where(visible[:, sl], take, 0.0)
        take_wide = jnp.dot(take.astype(BF16), spread, preferred_element_type=F32)
        if c < n_blocks - 1:
            sel_ref[:, c * wide:(c + 1) * wide] = take_wide
        else:
            sel_ref[:, c * wide:c * wide + NEW_PAD] = take_wide[:, :NEW_PAD]
        seen = seen + jnp.sum(eqc, axis=1, keepdims=True)


def _sample_select(scores2d, topk, n_past, n_new, rows):
    n, lk = scores2d.shape
    assert n % rows == 0 and rows % n_new == 0 and lk == n_past + NEW_PAD
    lk_wide = n_past * A_HEADS + NEW_PAD
    return pl.pallas_call(
        functools.partial(_sample_select_kernel, topk=topk, n_past=n_past, n_new=n_new),
        grid=(n // rows,),
        in_specs=[pl.BlockSpec((rows, lk), lambda i: (i, 0))],
        out_specs=pl.BlockSpec((rows, lk_wide), lambda i: (i, 0)),
        out_shape=jax.ShapeDtypeStruct((n, lk_wide), F32),
        compiler_params=_cparams(("arbitrary",)),
        name="sample_select",
    )(scores2d)


def _sample_attend_kernel(pt_ref, q_ref, sel_ref, bias_ref, kn_ref, vn_ref, ck_ref, cv_ref, o_ref,
                          kbuf, vbuf, ksem, vsem, *, n_pages, n_new):
    b = pl.program_id(0)
    n_chunks = n_pages // PAGES_PER_CHUNK
    rows_per_chunk = PAGES_PER_CHUNK * PAGE_SIZE * A_HEADS
    rows = n_new * A_HEADS

    def chunk_copies(seq, c, sl):
        cps = []
        for p in range(PAGES_PER_CHUNK):
            page = pt_ref[seq, c * PAGES_PER_CHUNK + p]
            cps.append(pltpu.make_async_copy(ck_ref.at[page], kbuf.at[sl, p], ksem.at[sl]))
            cps.append(pltpu.make_async_copy(cv_ref.at[page], vbuf.at[sl, p], vsem.at[sl]))
        return cps

    @pl.when(b == 0)
    def _():
        for cp in chunk_copies(0, 0, 0):
            cp.start()

    q = q_ref[0]
    nt = (((1,), (1,)), ((), ()))

    def softmax_step(state, kmat, vmat, lanes):
        m_old, l_old, acc = state
        logits = lax.dot_general(q, kmat, nt, preferred_element_type=F32)
        sel = sel_ref[0, :, lanes]
        sel_rows = jnp.concatenate([jnp.broadcast_to(sel[t:t + 1, :], (A_HEADS, sel.shape[1])) for t in range(n_new)],
                                   axis=0)
        x = jnp.where(sel_rows > 0.5, logits + bias_ref[:, lanes], NEG_BIG)
        m_new = jnp.maximum(m_old, jnp.max(x, axis=1, keepdims=True))
        alpha = jnp.exp(m_old - m_new)
        pm = jnp.exp(x - m_new)
        l_new = alpha * l_old + jnp.sum(pm, axis=1, keepdims=True)
        acc = alpha * acc + jnp.dot(pm.astype(BF16), vmat, preferred_element_type=F32)
        return m_new, l_new, acc

    def chunk_body(c, state):
        sl = c % 2

        @pl.when(c + 1 < n_chunks)
        def _():
            for cp in chunk_copies(b, c + 1, 1 - sl):
                cp.start()

        @pl.when((c + 1 == n_chunks) & (b + 1 < pl.num_programs(0)))
        def _():
            for cp in chunk_copies(b + 1, 0, 1 - sl):
                cp.start()

        for cp in chunk_copies(b, c, sl):
            cp.wait()
        kc = kbuf[sl].reshape(rows_per_chunk, A_HEAD_DIM).astype(BF16)
        vc = vbuf[sl].reshape(rows_per_chunk, A_HEAD_DIM).astype(BF16)
        lanes = pl.ds(pl.multiple_of(c * rows_per_chunk, rows_per_chunk), rows_per_chunk)
        return softmax_step(state, kc, vc, lanes)

    state = (jnp.full((rows, 1), NEG_BIG, F32), jnp.zeros((rows, 1), F32), jnp.zeros((rows, A_HEAD_DIM), F32))
    state = lax.fori_loop(0, n_chunks, chunk_body, state)
    first_new = n_pages * PAGE_SIZE * A_HEADS
    _, l_fin, acc = softmax_step(state, kn_ref[0], vn_ref[0], slice(first_new, first_new + NEW_PAD))
    o_ref[0] = (acc / l_fin).astype(o_ref.dtype)


def _sample_attend(page_table, q_rows, sel_wide, bias_wide, k_new_rows, v_new_rows, cache_k, cache_v):
    nb, n_pages = page_table.shape
    rows = q_rows.shape[1]
    n_new = rows // A_HEADS
    assert n_pages % (2 * PAGES_PER_CHUNK) == 0
    assert cache_k.shape[1:] == (PAGE_SIZE, A_HEADS, A_HEAD_DIM)
    lk_wide = n_pages * PAGE_SIZE * A_HEADS + NEW_PAD
    page_buf = pltpu.VMEM((2, PAGES_PER_CHUNK, PAGE_SIZE, A_HEADS, A_HEAD_DIM), F32)
    grid_spec = pltpu.PrefetchScalarGridSpec(
        num_scalar_prefetch=1,
        grid=(nb,),
        in_specs=[pl.BlockSpec((1, rows, A_HEAD_DIM), lambda b, pt: (b, 0, 0)),
                  pl.BlockSpec((1, n_new, lk_wide), lambda b, pt: (b, 0, 0)),
                  pl.BlockSpec((rows, lk_wide), lambda b, pt: (0, 0), pipeline_mode=pl.Buffered(1)),
                  pl.BlockSpec((1, NEW_PAD, A_HEAD_DIM), lambda b, pt: (b, 0, 0)),
                  pl.BlockSpec((1, NEW_PAD, A_HEAD_DIM), lambda b, pt: (b, 0, 0)),
                  pl.BlockSpec(memory_space=pl.ANY),
                  pl.BlockSpec(memory_space=pl.ANY)],
        out_specs=pl.BlockSpec((1, rows, A_HEAD_DIM), lambda b, pt: (b, 0, 0)),
        scratch_shapes=[page_buf, page_buf, pltpu.SemaphoreType.DMA((2,)), pltpu.SemaphoreType.DMA((2,))],
    )
    return pl.pallas_call(
        functools.partial(_sample_attend_kernel, n_pages=n_pages, n_new=n_new),
        grid_spec=grid_spec,
        out_shape=jax.ShapeDtypeStruct((nb, rows, A_HEAD_DIM), BF16),
        compiler_params=_cparams(("arbitrary",)),
        name="sample_attend",
    )(page_table, q_rows, sel_wide, bias_wide, k_new_rows, v_new_rows, cache_k, cache_v)


def _dsa_sample(qa_bf, kab, vab, qi_bf, kiw, rel_bias, cache_k, cache_v, cache_ki, page_table, n_new):
    nb, n_pages = page_table.shape
    n_past = n_pages * PAGE_SIZE
    new_keys = NEW_PAD // A_HEADS
    assert n_new <= new_keys and n_past >= REL_MAX_DIST
    topk = min(INDEX_TOPK, (n_past + n_new) // 4)
    qi_rows = qi_bf.reshape(nb, n_new, IDX_HEADS, IDX_DIM).transpose(0, 2, 1, 3).reshape(nb, IDX_HEADS * n_new, IDX_DIM)
    wi = kiw[:, IDX_DIM:IDX_DIM + IDX_HEADS] * INDEX_SCALE
    w_rows = jnp.broadcast_to(wi.reshape(nb, n_new, IDX_HEADS).transpose(0, 2, 1).reshape(nb, IDX_HEADS * n_new, 1),
                              (nb, IDX_HEADS * n_new, LANES))
    pad_rows = lambda a: jnp.pad(a, ((0, 0), (0, NEW_PAD - a.shape[1]), (0, 0)))
    ki_new = pad_rows(kiw[:, :IDX_DIM].astype(BF16).reshape(nb, n_new, IDX_DIM))
    scores = _sample_scores(page_table, qi_rows, w_rows, ki_new, cache_ki)
    lk = scores.shape[-1]
    sel_wide = _sample_select(scores.reshape(nb * n_new, lk), topk, n_past, n_new, rows=32)
    n_tail = REL_MAX_DIST + new_keys
    table = _bias_by_distance(rel_bias, REL_MAX_DIST + n_tail)
    dist = (n_past + jnp.arange(n_new))[:, None] - (n_past - REL_MAX_DIST + jnp.arange(n_tail))[None, :]
    tail = table[:, jnp.clip(dist, 0, table.shape[1] - 1)].transpose(1, 0, 2)
    far = jnp.broadcast_to(table[None, :, -1:], (n_new, A_HEADS, n_past - REL_MAX_DIST))
    per_key = jnp.concatenate([far, tail], axis=2).reshape(n_new * A_HEADS, n_past + new_keys)
    same_head = (jnp.arange(n_new * A_HEADS) % A_HEADS)[:, None, None] == jnp.arange(A_HEADS)[None, None, :]
    bias_wide = jnp.where(same_head, per_key[:, :, None], NEG_BIG).reshape(n_new * A_HEADS, -1)
    o = _sample_attend(page_table, qa_bf.reshape(nb, n_new * A_HEADS, A_HEAD_DIM),
                       sel_wide.reshape(nb, n_new, -1), bias_wide,
                       pad_rows(kab.reshape(nb, n_new * A_HEADS, A_HEAD_DIM)),
                       pad_rows(vab.reshape(nb, n_new * A_HEADS, A_HEAD_DIM)),
                       cache_k, cache_v)
    return o.reshape(nb * n_new, A_WIDTH)


def kernel(x_prompt, x_sample, cache_k, cache_v, cache_idx_k, state_gla, page_table, norm_mix, w_in, w_gate_up, b_gate, gla_norm, w_branch_a, w_branch_b, w_out, norm_ffn, w_router, b_router, w_exp_up, b_exp_up, w_exp_down, b_exp_down, rel_bias, norm_final):
    batch, seq, d = x_prompt.shape
    nb, n_new, _ = x_sample.shape
    assert w_in.shape[0] == 1, "the final norm is fused into the single layer's MoE kernel"
    layer = 0
    w_packed = _pack_w_in(w_in[layer])

    def trunk(x2d, attend, gla_args, tm_moe):
        tokens = x2d.shape[0]
        tm = min(512, tokens)
        qa, ka, kab, va, vab, qi, kiw, qb, kb, vb, r, glow, ga, gb = _inproj(x2d, norm_mix[layer], w_packed, tm)
        oa = attend(qa, kab, vab, qi, kiw)
        ob, state = gla_args(qb, kb, vb, r, glow)
        x2, h2, comb = _merge(oa, ob, ga, gb, x2d, w_branch_a[layer], w_branch_b[layer], w_out[layer], norm_ffn[layer],
                              w_router[layer], b_router[layer], tm)
        y = _moe(h2, comb, x2, w_exp_up[layer], b_exp_up[layer], w_exp_down[layer], b_exp_down[layer], norm_final,
                 min(tm_moe, tokens))
        return y, ka, va, kiw[:, :IDX_DIM], state

    gla_w = (w_gate_up[layer], b_gate[layer], gla_norm[layer])

    def gla_prompt(qb, kb, vb, r, glow):
        s0 = jnp.zeros((batch, B_HEADS, B_KEY_DIM, B_VAL_DIM), F32)
        return _gla(qb, kb, vb, r, glow, *gla_w, s0, batch, seq, GLA_CHUNK)

    def gla_sample(qb, kb, vb, r, glow):
        pad = lambda a: jnp.pad(a.reshape(nb, n_new, -1), ((0, 0), (0, GLA_CHUNK - n_new), (0, 0))).reshape(nb * GLA_CHUNK, -1)
        o, state = _gla(pad(qb), pad(kb), pad(vb), pad(r), pad(glow), *gla_w, state_gla[layer].astype(F32), nb, GLA_CHUNK,
                        n_new)
        return o.reshape(nb, GLA_CHUNK, -1)[:, :n_new].reshape(nb * n_new, -1), state

    yp, kp, vp, kip, sp = trunk(
        x_prompt.reshape(batch * seq, d),
        lambda qa, kab, vab, qi, kiw: _dsa_prompt(qa, kab, vab, qi, kiw, rel_bias, batch, seq),
        gla_prompt, 1024)
    ys, ks, vs, kis, ss = trunk(
        x_sample.reshape(nb * n_new, d),
        lambda qa, kab, vab, qi, kiw: _dsa_sample(qa, kab, vab, qi, kiw, rel_bias, cache_k[layer], cache_v[layer],
                                                  cache_idx_k[layer], page_table, n_new),
        gla_sample, 512)
    heads = (A_HEADS, A_HEAD_DIM)
    return (yp.reshape(batch, seq, d), ys.reshape(nb, n_new, d),
            kp.reshape(1, batch, seq, *heads), vp.reshape(1, batch, seq, *heads), kip.reshape(1, batch, seq, IDX_DIM),
            sp.astype(x_prompt.dtype)[None],
            ks.reshape(1, nb, n_new, *heads), vs.reshape(1, nb, n_new, *heads), kis.reshape(1, nb, n_new, IDX_DIM),
            ss.astype(state_gla.dtype)[None])
```

```python
import functools
import math

import numpy as np
import jax
import jax.numpy as jnp
from jax import lax
from jax.experimental import pallas as pl
from jax.experimental.pallas import tpu as pltpu

F32 = jnp.float32
BF16 = jnp.bfloat16

A_HEADS = 8
A_HEAD_DIM = 64
A_WIDTH = A_HEADS * A_HEAD_DIM
IDX_HEADS = 4
IDX_DIM = 64
INDEX_TOPK = 256
INDEX_SCALE = (IDX_HEADS * IDX_DIM) ** -0.5
REL_BUCKETS = 32
REL_MAX_DIST = 128
B_HEADS = 4
B_KEY_DIM = 64
B_VAL_DIM = 128
B_QK_WIDTH = B_HEADS * B_KEY_DIM
B_V_WIDTH = B_HEADS * B_VAL_DIM
GATE_RANK = 16
GATE_TAU = 16.0
GLA_CHUNK = 64
TOP_K = 4
SWIGLU_ALPHA = 1.702
SWIGLU_LIMIT = 7.0
EPS = 1e-6
PAGE_SIZE = 128

LANES = 128
NEG_BIG = -1e30
INT_MIN = -2 ** 31
VMEM_LIMIT = 56 * 1024 * 1024


def _cparams(sem):
    return pltpu.CompilerParams(dimension_semantics=sem, vmem_limit_bytes=VMEM_LIMIT)


def _rms(xf, g):
    return xf * lax.rsqrt(jnp.mean(xf * xf, axis=-1, keepdims=True) + EPS) * g


_C_QA, _C_KA, _C_VA = 0, 512, 1024
_C_QI, _C_KIW = 1536, 1792
_C_QB, _C_KB, _C_VB, _C_R, _C_GLOW = 1920, 2176, 2432, 2944, 3456
_C_GA, _C_GB = 3584, 4608
_C_END = 5632


def _pack_w_in(w_in):
    d = w_in.shape[0]
    o = np.cumsum([0, 512, 512, 512, 256, 64, 4, 256, 256, 512, 16, 512, 1024, 1024]).tolist()
    seg = lambda i: w_in[:, o[i]:o[i + 1]]
    z = lambda n: jnp.zeros((d, n), w_in.dtype)
    packed = jnp.concatenate(
        [seg(0), seg(1), seg(2), seg(3), seg(4), seg(5), z(60),
         seg(6), seg(7), seg(8), seg(10), seg(9), z(112), seg(11), seg(12)], axis=1)
    assert packed.shape[1] == _C_END
    return packed.astype(BF16)


def _inproj_kernel(x_ref, g_ref, w_ref, qa_ref, ka_ref, kab_ref, va_ref, vab_ref, qi_ref, kiw_ref,
                   qb_ref, kb_ref, vb_ref, r_ref, glow_ref, ga_ref, gb_ref):
    h = _rms(x_ref[...], g_ref[...]).astype(BF16)

    def proj(a, b):
        return jnp.dot(h, w_ref[:, a:b], preferred_element_type=F32)

    za = proj(_C_QA, _C_QI)
    qa_ref[...] = (za[:, 0:512] * (A_HEAD_DIM ** -0.5)).astype(BF16)
    ka = za[:, 512:1024]
    va = za[:, 1024:1536]
    ka_ref[...] = ka
    kab_ref[...] = ka.astype(BF16)
    va_ref[...] = va
    vab_ref[...] = va.astype(BF16)
    zi = proj(_C_QI, _C_QB)
    qi_ref[...] = zi[:, 0:256].astype(BF16)
    kiw_ref[...] = zi[:, 256:384]
    zb = proj(_C_QB, _C_GA)
    qb_ref[...] = zb[:, 0:256]
    kb_ref[...] = zb[:, 256:512]
    vb_ref[...] = zb[:, 512:1024]
    r_ref[...] = zb[:, 1024:1536]
    glow_ref[...] = zb[:, 1536:1664]
    zg = proj(_C_GA, _C_END)
    ga_ref[...] = zg[:, 0:1024]
    gb_ref[...] = zg[:, 1024:2048]


def _inproj(x2d, g, w_packed, tm):
    t, d = x2d.shape
    widths = [(512, BF16), (512, F32), (512, BF16), (512, F32), (512, BF16), (256, BF16), (128, F32),
              (256, F32), (256, F32), (512, F32), (512, F32), (128, F32), (1024, F32), (1024, F32)]
    return pl.pallas_call(
        _inproj_kernel,
        grid=(t // tm,),
        in_specs=[pl.BlockSpec((tm, d), lambda i: (i, 0)),
                  pl.BlockSpec((1, d), lambda i: (0, 0)),
                  pl.BlockSpec((d, _C_END), lambda i: (0, 0))],
        out_specs=[pl.BlockSpec((tm, w), lambda i: (i, 0)) for w, _ in widths],
        out_shape=[jax.ShapeDtypeStruct((t, w), dt) for w, dt in widths],
        compiler_params=_cparams(("arbitrary",)),
        name="inproj",
    )(x2d, g.reshape(1, d), w_packed)


def _sortable_key(s):
    bits = pltpu.bitcast(s, jnp.int32)
    key = bits ^ ((bits >> 31) & 0x7FFFFFFF)
    return jnp.where(key == -1, 0, key)


def _t5_bucket_np(dist):
    max_exact = REL_BUCKETS // 2
    ratio = np.log(np.maximum(dist, 1).astype(np.float32) / np.float32(max_exact)) / np.float32(
        math.log(REL_MAX_DIST / max_exact))
    large = np.minimum(max_exact + (ratio * (REL_BUCKETS - max_exact)).astype(np.int32), REL_BUCKETS - 1)
    return np.where(dist < max_exact, dist, large).astype(np.int32)


def _bias_by_distance(rel_bias, n):
    dist = jnp.arange(n, dtype=jnp.int32)
    max_exact = REL_BUCKETS // 2
    log_ratio = jnp.log(jnp.maximum(dist, 1).astype(F32) / max_exact) / math.log(REL_MAX_DIST / max_exact)
    large = jnp.minimum(max_exact + (log_ratio * (REL_BUCKETS - max_exact)).astype(jnp.int32), REL_BUCKETS - 1)
    bucket = jnp.where(dist < max_exact, dist, large)
    return rel_bias[bucket].T.astype(F32)


def _dsa_prompt_kernel(rel_ref, qT_ref, qiT_ref, wT_ref, k_ref, vT_ref, ki_ref, bkt_ref, o_ref,
                       s_scr, qbd_scr, acc_scr, m_scr, l_scr, bias_scr, mask_scr, *, tq, tk, topk):
    i = pl.program_id(1)

    @pl.when((pl.program_id(0) == 0) & (i == 0))
    def _():
        for which in range(2 * (tk // tq)):
            bkt = bkt_ref[which]

            def one_head(h, carry):
                far = rel_ref[REL_BUCKETS - 1, h]
                bias_scr[which, h] = lax.fori_loop(
                    0, REL_BUCKETS - 1, lambda b, t: jnp.where(bkt == b, rel_ref[b, h] - far, t),
                    jnp.zeros((tk, tq), F32))
                return carry

            lax.fori_loop(0, A_HEADS, one_head, 0)

    diag = (i * tq) // tk
    nkb = diag + 1
    qpos = i * tq + lax.broadcasted_iota(jnp.int32, (1, tq), 1)
    krow = lax.broadcasted_iota(jnp.int32, (tk, 1), 0)

    qiT = qiT_ref[0]
    qi_st = jnp.concatenate([qiT[h * IDX_DIM:(h + 1) * IDX_DIM, :] for h in range(IDX_HEADS)], axis=1)
    w = wT_ref[0]

    def score_block(j, carry):
        s = jnp.dot(ki_ref[0, j], qi_st, preferred_element_type=F32)
        acc = jnp.maximum(s[:, 0:tq], 0.0) * w[0:1, :]
        for h in range(1, IDX_HEADS):
            acc = acc + jnp.maximum(s[:, h * tq:(h + 1) * tq], 0.0) * w[h:h + 1, :]
        acc = jnp.where(j * tk + krow <= qpos, acc, -jnp.inf)
        s_scr[pl.ds(pl.multiple_of(j * tk, tk), tk), :] = _sortable_key(acc)
        return carry

    lax.fori_loop(0, nkb, score_block, 0)

    def count_ge(cand):
        def body(j, c):
            blk = s_scr[pl.ds(pl.multiple_of(j * tk, tk), tk), :]
            hit = jnp.where(blk >= cand, 1, 0).astype(jnp.int32)
            return c + jnp.sum(hit.reshape(tk // 8, 8, tq), axis=0)
        c = lax.fori_loop(0, nkb, body, jnp.zeros((8, tq), jnp.int32))
        return jnp.sum(c, axis=0, keepdims=True)

    def bisect(step, thr):
        cand = thr + lax.shift_left(jnp.int32(1), 31 - step)
        return jnp.where(count_ge(cand) >= topk, cand, thr)

    thr = lax.fori_loop(0, 32, bisect, jnp.full((1, tq), INT_MIN, jnp.int32))
    n_take = (topk - count_ge(thr + 1)).astype(F32)

    rowid = lax.broadcasted_iota(jnp.int32, (2 * A_HEAD_DIM, tq), 0)
    for p in range(A_HEADS // 2):
        slab = qT_ref[0, p * 128:(p + 1) * 128, :].astype(F32)
        top = jnp.where(rowid < A_HEAD_DIM, slab, 0.0)
        qbd_scr[p] = jnp.concatenate([top, slab - top], axis=1).astype(BF16)

    acc_scr[...] = jnp.zeros_like(acc_scr)
    m_scr[...] = jnp.full_like(m_scr, NEG_BIG)
    l_scr[...] = jnp.zeros_like(l_scr)
    ri = lax.broadcasted_iota(jnp.int32, (tk, tk), 0)
    ci = lax.broadcasted_iota(jnp.int32, (tk, tk), 1)
    ltri = jnp.where(ci < ri, 1.0, 0.0).astype(BF16)

    def attend_block(j, tie_seen, near):
        keyblk = s_scr[pl.ds(pl.multiple_of(j * tk, tk), tk), :]
        eq = jnp.where(keyblk == thr, 1.0, 0.0)
        before = tie_seen + jnp.dot(ltri, eq.astype(BF16), preferred_element_type=F32)
        take = jnp.where(keyblk > thr, 1.0, jnp.where(before < n_take, eq, 0.0))
        if near:
            take = jnp.where(j * tk + krow <= qpos, take, 0.0)
            which = 2 * ((i * tq) % tk // tq) + (diag - j)
        mask_scr[...] = jnp.where(take > 0.5, 0.0, NEG_BIG)
        for p in range(A_HEADS // 2):
            lg = jnp.dot(k_ref[0, j, :, p * 128:(p + 1) * 128], qbd_scr[p], preferred_element_type=F32)
            for u in range(2):
                h = 2 * p + u
                x = lg[:, u * tq:(u + 1) * tq] + mask_scr[...]
                if near:
                    x = x + bias_scr[which, h]
                m_old = m_scr[h:h + 1, :]
                m_new = jnp.maximum(m_old, jnp.max(x, axis=0, keepdims=True))
                alpha = jnp.exp(m_old - m_new)
                pm = jnp.exp(x - m_new)
                l_scr[h:h + 1, :] = alpha * l_scr[h:h + 1, :] + jnp.sum(pm, axis=0, keepdims=True)
                pv = jnp.dot(vT_ref[0, j, h * A_HEAD_DIM:(h + 1) * A_HEAD_DIM, :], pm.astype(BF16),
                             preferred_element_type=F32)
                rows = slice(h * A_HEAD_DIM, (h + 1) * A_HEAD_DIM)
                acc_scr[rows, :] = alpha * acc_scr[rows, :] + pv
                m_scr[h:h + 1, :] = m_new
        return tie_seen + jnp.sum(eq, axis=0, keepdims=True)

    n_far = jnp.maximum(diag - 1, 0)
    tie_seen = lax.fori_loop(0, n_far, functools.partial(attend_block, near=False), jnp.zeros((1, tq), F32))
    lax.fori_loop(n_far, nkb, functools.partial(attend_block, near=True), tie_seen)

    for h in range(A_HEADS):
        rows = slice(h * A_HEAD_DIM, (h + 1) * A_HEAD_DIM)
        acc_scr[rows, :] = acc_scr[rows, :] / l_scr[h:h + 1, :]
    o_ref[0] = acc_scr[...].T.astype(o_ref.dtype)


def _dsa_prompt(qa_bf, ka_bf, va_bf, qi_bf, kiw, rel_bias, batch, seq):
    tq, tk = 256, 256
    assert seq % tk == 0 and tk % tq == 0
    topk = min(INDEX_TOPK, seq // 4)
    nk = seq // tk
    qT = qa_bf.reshape(batch, seq, A_WIDTH).transpose(0, 2, 1)
    qiT = qi_bf.reshape(batch, seq, IDX_HEADS * IDX_DIM).transpose(0, 2, 1)
    wi = kiw[:, IDX_DIM:IDX_DIM + IDX_HEADS] * INDEX_SCALE
    wT = jnp.pad(wi.reshape(batch, seq, IDX_HEADS).transpose(0, 2, 1), ((0, 0), (0, 8 - IDX_HEADS), (0, 0)))
    k4 = ka_bf.reshape(batch, nk, tk, A_WIDTH)
    vT4 = va_bf.reshape(batch, nk, tk, A_WIDTH).transpose(0, 1, 3, 2)
    ki4 = kiw[:, :IDX_DIM].astype(BF16).reshape(batch, nk, tk, IDX_DIM)
    assert REL_MAX_DIST <= tk
    d0 = np.arange(tq)[None, :] - np.arange(tk)[:, None]
    bkt = jnp.asarray(np.stack([_t5_bucket_np(np.maximum(d0 + off * tq + kind * tk, 0))
                                for off in range(tk // tq) for kind in range(2)]), jnp.int32)

    once = dict(pipeline_mode=pl.Buffered(1))
    kern = functools.partial(_dsa_prompt_kernel, tq=tq, tk=tk, topk=topk)
    out = pl.pallas_call(
        kern,
        grid=(batch, seq // tq),
        in_specs=[pl.BlockSpec(memory_space=pltpu.SMEM),
                  pl.BlockSpec((1, A_WIDTH, tq), lambda b, i: (b, 0, i)),
                  pl.BlockSpec((1, IDX_HEADS * IDX_DIM, tq), lambda b, i: (b, 0, i)),
                  pl.BlockSpec((1, 8, tq), lambda b, i: (b, 0, i)),
                  pl.BlockSpec((1, nk, tk, A_WIDTH), lambda b, i: (b, 0, 0, 0), **once),
                  pl.BlockSpec((1, nk, A_WIDTH, tk), lambda b, i: (b, 0, 0, 0), **once),
                  pl.BlockSpec((1, nk, tk, IDX_DIM), lambda b, i: (b, 0, 0, 0), **once),
                  pl.BlockSpec(bkt.shape, lambda b, i: (0, 0, 0), **once)],
        out_specs=pl.BlockSpec((1, tq, A_WIDTH), lambda b, i: (b, i, 0)),
        out_shape=jax.ShapeDtypeStruct((batch, seq, A_WIDTH), BF16),
        scratch_shapes=[pltpu.VMEM((seq, tq), jnp.int32),
                        pltpu.VMEM((A_HEADS // 2, 2 * A_HEAD_DIM, 2 * tq), BF16),
                        pltpu.VMEM((A_WIDTH, tq), F32),
                        pltpu.VMEM((A_HEADS, tq), F32),
                        pltpu.VMEM((A_HEADS, tq), F32),
                        pltpu.VMEM((bkt.shape[0], A_HEADS, tk, tq), F32),
                        pltpu.VMEM((tk, tq), F32)],
        compiler_params=_cparams(("arbitrary", "arbitrary")),
        name="dsa_prompt",
    )(rel_bias.astype(F32), qT, qiT, wT, k4, vT4, ki4, bkt)
    return out.reshape(batch * seq, A_WIDTH)


def _gla_kernel(q_ref, k_ref, v_ref, r_ref, glow_ref, wg_ref, bg_ref, gn_ref, s0_ref, o_ref, s_out_ref,
                sT_ref, q_scr, b_scr, att_scr, *, n_valid):
    C = GLA_CHUNK
    hp = lax.Precision.HIGHEST

    @pl.when(pl.program_id(1) == 0)
    def _():
        for p in range(B_HEADS // 2):
            sT_ref[p] = s0_ref[0, 2 * p:2 * p + 2].reshape(LANES, B_VAL_DIM).T

    g = jnp.dot(glow_ref[0], wg_ref[...], precision=hp, preferred_element_type=F32) + bg_ref[...]
    log_a = (jnp.minimum(g, 0.0) - jnp.log1p(jnp.exp(-jnp.abs(g)))) * (1.0 / GATE_TAU)
    if n_valid < C:
        log_a = jnp.where(lax.broadcasted_iota(jnp.int32, (C, 1), 0) < n_valid, log_a, 0.0)
        att_scr[...] = jnp.zeros_like(att_scr)
    tri = jnp.where(lax.broadcasted_iota(jnp.int32, (C, C), 1) <= lax.broadcasted_iota(jnp.int32, (C, C), 0), 1.0, 0.0)
    b = jnp.dot(tri, log_a, precision=hp, preferred_element_type=F32)
    q = q_ref[0] * (B_KEY_DIM ** -0.5)
    k = k_ref[0]
    v = v_ref[0]
    q_scr[...] = q
    b_scr[...] = b
    eb = jnp.exp(b)
    qe = q * eb
    kdec = k * jnp.exp(b[C - 1:C, :] - b)
    eb_last = eb[C - 1:C, :]

    lane1 = lax.broadcasted_iota(jnp.int32, (1, LANES), 1)
    head_mask = (jnp.where(lane1 < B_KEY_DIM, 1.0, 0.0), jnp.where(lane1 >= B_KEY_DIM, 1.0, 0.0))
    lane_c = lax.broadcasted_iota(jnp.int32, (C, LANES), 1) % B_KEY_DIM
    row_c = lax.broadcasted_iota(jnp.int32, (C, LANES), 0)
    diag = jnp.where(lane_c == row_c, 1.0, 0.0)
    causal = jnp.where(lane_c <= row_c, 1.0, 0.0)
    same_head = jnp.where(lax.broadcasted_iota(jnp.int32, (LANES, LANES), 0) // B_KEY_DIM
                          == lax.broadcasted_iota(jnp.int32, (LANES, LANES), 1) // B_KEY_DIM, 1.0, 0.0).astype(BF16)
    tn = (((0,), (0,)), ((), ()))
    nt = (((1,), (1,)), ((), ()))

    for p in range(B_HEADS // 2):
        sl = slice(p * LANES, (p + 1) * LANES)
        kp = k[:, sl]
        bp = b[:, sl]

        def intra_rows(grp, carry):
            base = pl.multiple_of(grp * 8, 8)
            b8 = b_scr[pl.ds(base, 8), sl]
            q8 = q_scr[pl.ds(base, 8), sl]
            out = []
            for rr in range(8):
                if rr >= n_valid:
                    out.append(jnp.zeros((1, LANES), F32))
                    continue
                x = (q8[rr:rr + 1, :] * kp) * jnp.exp(jnp.minimum(b8[rr:rr + 1, :] - bp, 0.0))
                xh = x.astype(BF16)
                xl = (x - xh.astype(F32)).astype(BF16)
                seg = (jnp.dot(xh, same_head, preferred_element_type=F32)
                       + jnp.dot(xl, same_head, preferred_element_type=F32))
                out.append(jnp.sum(seg * diag, axis=0, keepdims=True))
            att_scr[pl.ds(base, 8), sl] = jnp.concatenate(out, axis=0)
            return carry

        lax.fori_loop(0, (n_valid + 7) // 8, intra_rows, 0)
        att = att_scr[:, sl] * causal
        sT = sT_ref[p]
        sT_bf = sT.astype(BF16)
        v_pair = jnp.concatenate([v[:, (2 * p) * B_VAL_DIM:(2 * p + 1) * B_VAL_DIM],
                                  v[:, (2 * p + 1) * B_VAL_DIM:(2 * p + 2) * B_VAL_DIM]], axis=0).astype(BF16)
        upd = jnp.zeros((B_VAL_DIM, LANES), F32)
        for u in range(2):
            h = 2 * p + u
            hs = slice(h * B_VAL_DIM, (h + 1) * B_VAL_DIM)
            o = lax.dot_general((qe[:, sl] * head_mask[u]).astype(BF16), sT_bf, nt, preferred_element_type=F32)
            o = o + jnp.dot((att * head_mask[u]).astype(BF16), v_pair, preferred_element_type=F32)
            o = o * lax.rsqrt(jnp.mean(o * o, axis=-1, keepdims=True) + EPS)
            rh = r_ref[0, :, hs]
            o_ref[0, :, hs] = (o * gn_ref[:, hs] * (rh * jax.nn.sigmoid(rh))).astype(o_ref.dtype)
            upd = upd + lax.dot_general(v[:, hs].astype(BF16), (kdec[:, sl] * head_mask[u]).astype(BF16), tn,
                                        preferred_element_type=F32)
        s_new = sT * eb_last[:, sl] + upd
        sT_ref[p] = s_new

        @pl.when(pl.program_id(1) == pl.num_programs(1) - 1)
        def _():
            s_out_ref[0, 2 * p:2 * p + 2] = s_new.T.reshape(2, B_KEY_DIM, B_VAL_DIM)


def _gla(qb, kb, vb, r, glow, w_gate_up, b_gate, gla_norm, s0, nb, length, n_valid):
    C = GLA_CHUNK
    assert length % C == 0 and B_KEY_DIM * 2 == LANES and C == B_KEY_DIM and B_VAL_DIM == LANES
    rs = lambda a: a.reshape(nb, length, a.shape[-1])
    wg = jnp.pad(w_gate_up, ((0, LANES - GATE_RANK), (0, 0)))
    tok = lambda w: pl.BlockSpec((1, C, w), lambda b, c: (b, c, 0))
    full = lambda shape: pl.BlockSpec(shape, lambda b, c: (0,) * len(shape))
    st = pl.BlockSpec((1, B_HEADS, B_KEY_DIM, B_VAL_DIM), lambda b, c: (b, 0, 0, 0))
    o, state = pl.pallas_call(
        functools.partial(_gla_kernel, n_valid=n_valid),
        grid=(nb, length // C),
        in_specs=[tok(B_QK_WIDTH), tok(B_QK_WIDTH), tok(B_V_WIDTH), tok(B_V_WIDTH), tok(LANES),
                  full((LANES, B_QK_WIDTH)), full((1, B_QK_WIDTH)), full((1, B_V_WIDTH)), st],
        out_specs=[tok(B_V_WIDTH), st],
        out_shape=[jax.ShapeDtypeStruct((nb, length, B_V_WIDTH), BF16),
                   jax.ShapeDtypeStruct((nb, B_HEADS, B_KEY_DIM, B_VAL_DIM), F32)],
        scratch_shapes=[pltpu.VMEM((B_HEADS // 2, B_VAL_DIM, LANES), F32), pltpu.VMEM((C, B_QK_WIDTH), F32),
                        pltpu.VMEM((C, B_QK_WIDTH), F32), pltpu.VMEM((C, B_QK_WIDTH), F32)],
        compiler_params=_cparams(("arbitrary", "arbitrary")),
        name="gla",
    )(rs(qb), rs(kb), rs(vb), rs(r), rs(glow), wg, b_gate.reshape(1, -1), gla_norm.reshape(1, -1), s0)
    return o.reshape(nb * length, B_V_WIDTH), state


def _merge_kernel(oa_ref, ob_ref, ga_ref, gb_ref, x_ref, wa_ref, wb_ref, wo_ref, gf_ref, wr_ref, br_ref,
                  x2_ref, h2_ref, comb_ref, *, n_experts):
    ya = jnp.dot(oa_ref[...], wa_ref[...], preferred_element_type=F32)
    yb = jnp.dot(ob_ref[...], wb_ref[...], preferred_element_type=F32)
    merged = jax.nn.sigmoid(ga_ref[...]) * ya + jax.nn.sigmoid(gb_ref[...]) * yb
    x2 = x_ref[...] + jnp.dot(merged.astype(BF16), wo_ref[...], preferred_element_type=F32)
    x2_ref[...] = x2
    h2 = _rms(x2, gf_ref[...])
    h2_ref[...] = h2.astype(BF16)
    logits = jnp.dot(h2, wr_ref[...], precision=lax.Precision.HIGHEST, preferred_element_type=F32) + br_ref[...]
    lane = lax.broadcasted_iota(jnp.int32, logits.shape, 1).astype(F32)
    work = jnp.where(lane < n_experts, logits, -jnp.inf)
    picks = []
    for _ in range(TOP_K):
        m = jnp.max(work, axis=-1, keepdims=True)
        first = jnp.min(jnp.where(work == m, lane, float(LANES)), axis=-1, keepdims=True)
        hit = lane == first
        picks.append((m, hit))
        work = jnp.where(hit, -jnp.inf, work)
    es = [jnp.exp(m - picks[0][0]) for m, _ in picks]
    inv = 1.0 / (es[0] + es[1] + es[2] + es[3])
    comb = jnp.zeros_like(logits)
    for e, (_, hit) in zip(es, picks):
        comb = jnp.where(hit, e * inv, comb)
    comb_ref[...] = comb


def _merge(oa, ob, ga, gb, x2d, w_branch_a, w_branch_b, w_out, norm_ffn, w_router, b_router, tm):
    t, d = x2d.shape
    n_experts = w_router.shape[1]
    assert n_experts <= LANES and TOP_K == 4
    wr = jnp.pad(w_router, ((0, 0), (0, LANES - n_experts)))
    br = jnp.pad(b_router, (0, LANES - n_experts)).reshape(1, LANES)
    tok = lambda w: pl.BlockSpec((tm, w), lambda i: (i, 0))
    full = lambda a: pl.BlockSpec(a.shape, lambda i: (0,) * a.ndim)
    wa, wb, wo, gf = w_branch_a.astype(BF16), w_branch_b.astype(BF16), w_out.astype(BF16), norm_ffn.reshape(1, d)
    return pl.pallas_call(
        functools.partial(_merge_kernel, n_experts=n_experts),
        grid=(t // tm,),
        in_specs=[tok(A_WIDTH), tok(B_V_WIDTH), tok(d), tok(d), tok(d),
                  full(wa), full(wb), full(wo), full(gf), full(wr), full(br)],
        out_specs=[tok(d), tok(d), tok(LANES)],
        out_shape=[jax.ShapeDtypeStruct((t, d), F32), jax.ShapeDtypeStruct((t, d), BF16),
                   jax.ShapeDtypeStruct((t, LANES), F32)],
        compiler_params=_cparams(("arbitrary",)),
        name="merge",
    )(oa, ob, ga, gb, x2d, wa, wb, wo, gf, wr, br)


def _moe_kernel(h_ref, comb_ref, x2_ref, wug_ref, wul_ref, bug_ref, bul_ref, wd_ref, bd_ref, gfin_ref, o_ref):
    e = pl.program_id(1)
    half = pl.program_id(2)

    @pl.when((e == 0) & (half == 0))
    def _():
        o_ref[...] = jnp.zeros_like(o_ref)

    h = h_ref[...]
    glu = jnp.dot(h, wug_ref[0].astype(BF16), preferred_element_type=F32) + bug_ref[0]
    lin = jnp.dot(h, wul_ref[0].astype(BF16), preferred_element_type=F32) + bul_ref[0]
    glu = jnp.minimum(glu, SWIGLU_LIMIT)
    lin = jnp.clip(lin, -SWIGLU_LIMIT, SWIGLU_LIMIT)
    act = glu * jax.nn.sigmoid(SWIGLU_ALPHA * glu) * (lin + 1.0)
    y = jnp.dot(act.astype(BF16), wd_ref[0].astype(BF16), preferred_element_type=F32)
    y = y + jnp.where(half == 0, 1.0, 0.0) * bd_ref[0]
    lane = lax.broadcasted_iota(jnp.int32, comb_ref.shape, 1)
    c = jnp.sum(jnp.where(lane == e, comb_ref[...], 0.0), axis=-1, keepdims=True)
    o_ref[...] += c * y

    @pl.when((e == pl.num_programs(1) - 1) & (half == pl.num_programs(2) - 1))
    def _():
        o_ref[...] = _rms(x2_ref[...] + o_ref[...], gfin_ref[...])


def _moe(h2, comb, x2, w_up, b_up, w_down, b_down, norm_final, tm):
    t, d = x2.shape
    n_experts, _, two_de = w_up.shape
    de = two_de // 2
    nh = 2
    dh = de // nh
    b_up3 = b_up.reshape(n_experts, 1, two_de)
    b_down3 = b_down.reshape(n_experts, 1, d)
    tok = lambda w: pl.BlockSpec((tm, w), lambda i, e, f: (i, 0))
    return pl.pallas_call(
        _moe_kernel,
        grid=(t // tm, n_experts, nh),
        in_specs=[tok(d), tok(LANES), tok(d),
                  pl.BlockSpec((1, d, dh), lambda i, e, f: (e, 0, f)),
                  pl.BlockSpec((1, d, dh), lambda i, e, f: (e, 0, nh + f)),
                  pl.BlockSpec((1, 1, dh), lambda i, e, f: (e, 0, f)),
                  pl.BlockSpec((1, 1, dh), lambda i, e, f: (e, 0, nh + f)),
                  pl.BlockSpec((1, dh, d), lambda i, e, f: (e, f, 0)),
                  pl.BlockSpec((1, 1, d), lambda i, e, f: (e, 0, 0)),
                  pl.BlockSpec((1, d), lambda i, e, f: (0, 0))],
        out_specs=tok(d),
        out_shape=jax.ShapeDtypeStruct((t, d), F32),
        compiler_params=_cparams(("arbitrary", "arbitrary", "arbitrary")),
        name="moe",
    )(h2, comb, x2, w_up, w_up, b_up3, b_up3, w_down, b_down3, norm_final.reshape(1, d))


NEW_PAD = LANES
PAGES_PER_CHUNK = 8


def _sample_scores_kernel(pt_ref, qi_ref, w_ref, kin_ref, cki_ref, o_ref, buf, sem, *, n_pages, n_new):
    b = pl.program_id(0)
    slot = b % 2

    def page_copy(seq, p, sl):
        return pltpu.make_async_copy(cki_ref.at[pt_ref[seq, p]], buf.at[sl, p], sem.at[sl])

    @pl.when(b == 0)
    def _():
        for p in range(n_pages):
            page_copy(0, p, 0).start()

    @pl.when(b + 1 < pl.num_programs(0))
    def _():
        for p in range(n_pages):
            page_copy(b + 1, p, 1 - slot).start()

    for p in range(n_pages):
        page_copy(b, p, slot).wait()

    nt = (((1,), (1,)), ((), ()))
    qi = qi_ref[0]
    w = w_ref[0]
    half = IDX_HEADS * n_new // 2

    def reduce_heads(s, reps):
        wa = jnp.tile(w[0:half], (1, reps))
        wb = jnp.tile(w[half:2 * half], (1, reps))
        y = jnp.maximum(s[0:half], 0.0) * wa + jnp.maximum(s[half:2 * half], 0.0) * wb
        return (y + pltpu.roll(y, n_new, 0))[0:n_new]

    for c in range(n_pages // PAGES_PER_CHUNK):
        s = jnp.concatenate([jnp.dot(qi, buf[slot, c * PAGES_PER_CHUNK + p].astype(BF16), preferred_element_type=F32)
                             for p in range(PAGES_PER_CHUNK)], axis=1)
        o_ref[0, :, c * PAGES_PER_CHUNK * PAGE_SIZE:(c + 1) * PAGES_PER_CHUNK * PAGE_SIZE] = reduce_heads(
            s, PAGES_PER_CHUNK)
    s = lax.dot_general(qi, kin_ref[0], nt, preferred_element_type=F32)
    y = reduce_heads(s, 1)
    j = lax.broadcasted_iota(jnp.int32, (n_new, NEW_PAD), 1)
    t = lax.broadcasted_iota(jnp.int32, (n_new, NEW_PAD), 0)
    o_ref[0, :, n_pages * PAGE_SIZE:] = jnp.where(j <= t, y, -jnp.inf)


def _sample_scores(page_table, qi_rows, w_rows, ki_new_pad, cache_ki_t):
    nb, n_pages = page_table.shape
    n_new = qi_rows.shape[1] // IDX_HEADS
    assert n_new == 4 and n_pages % PAGES_PER_CHUNK == 0 and cache_ki_t.shape[1:] == (IDX_DIM, PAGE_SIZE)
    lk = n_pages * PAGE_SIZE + NEW_PAD
    grid_spec = pltpu.PrefetchScalarGridSpec(
        num_scalar_prefetch=1,
        grid=(nb,),
        in_specs=[pl.BlockSpec((1, IDX_HEADS * n_new, IDX_DIM), lambda b, pt: (b, 0, 0)),
                  pl.BlockSpec((1, IDX_HEADS * n_new, LANES), lambda b, pt: (b, 0, 0)),
                  pl.BlockSpec((1, NEW_PAD, IDX_DIM), lambda b, pt: (b, 0, 0)),
                  pl.BlockSpec(memory_space=pl.ANY)],
        out_specs=pl.BlockSpec((1, n_new, lk), lambda b, pt: (b, 0, 0)),
        scratch_shapes=[pltpu.VMEM((2, n_pages, IDX_DIM, PAGE_SIZE), F32), pltpu.SemaphoreType.DMA((2,))],
    )
    return pl.pallas_call(
        functools.partial(_sample_scores_kernel, n_pages=n_pages, n_new=n_new),
        grid_spec=grid_spec,
        out_shape=jax.ShapeDtypeStruct((nb, n_new, lk), F32),
        compiler_params=_cparams(("arbitrary",)),
        name="sample_scores",
    )(page_table, qi_rows, w_rows, ki_new_pad, cache_ki_t)


def _sample_select_kernel(s_ref, sel_ref, *, topk, n_past, n_new):
    rows, lk = s_ref.shape
    key = _sortable_key(s_ref[...])

    def count_ge(cand):
        return jnp.sum(jnp.where(key >= cand, 1.0, 0.0), axis=1, keepdims=True)

    def bisect(step, thr):
        cand = thr + lax.shift_left(jnp.int32(1), 31 - step)
        return jnp.where(count_ge(cand) >= topk, cand, thr)

    thr = lax.fori_loop(0, 32, bisect, jnp.full((rows, 1), INT_MIN, jnp.int32))
    n_take = topk - count_ge(thr + 1)
    col = lax.broadcasted_iota(jnp.int32, (rows, lk), 1)
    t = lax.broadcasted_iota(jnp.int32, (rows, lk), 0) % n_new
    visible = col - n_past <= t
    eq = jnp.where(key == thr, 1.0, 0.0)
    ra = lax.broadcasted_iota(jnp.int32, (LANES, LANES), 0)
    ca = lax.broadcasted_iota(jnp.int32, (LANES, LANES), 1)
    upper = jnp.where(ra < ca, 1.0, 0.0).astype(BF16)
    seen = jnp.zeros((rows, 1), F32)
    for c in range(lk // LANES):
        sl = slice(c * LANES, (c + 1) * LANES)
        eqc = eq[:, sl]
        before = seen + jnp.dot(eqc.astype(BF16), upper, preferred_element_type=F32)
        take = jnp.where(key[:, sl] > thr, 1.0, jnp.where(before < n_take, eqc, 0.0))
        sel_ref[:, sl] = jnp.where(visible[:, sl], take, 0.0)
        seen = seen + jnp.sum(eqc, axis=1, keepdims=True)


def _sample_select(scores2d, topk, n_past, n_new, rows):
    n, lk = scores2d.shape
    assert n % rows == 0 and rows % n_new == 0 and lk == n_past + NEW_PAD
    return pl.pallas_call(
        functools.partial(_sample_select_kernel, topk=topk, n_past=n_past, n_new=n_new),
        grid=(n // rows,),
        in_specs=[pl.BlockSpec((rows, lk), lambda i: (i, 0))],
        out_specs=pl.BlockSpec((rows, lk), lambda i: (i, 0)),
        out_shape=jax.ShapeDtypeStruct((n, lk), F32),
        compiler_params=_cparams(("arbitrary",)),
        name="sample_select",
    )(scores2d)


def _sample_attend_kernel(pt_ref, q_ref, sel_ref, bias_ref, kn_ref, vn_ref, ck_ref, cv_ref, o_ref,
                          kbuf, vbuf, ksem, vsem, *, n_pages, n_new):
    b = pl.program_id(0)
    n_chunks = n_pages // PAGES_PER_CHUNK
    keys_per_chunk = PAGES_PER_CHUNK * PAGE_SIZE
    rows = n_new * A_HEADS

    def chunk_copies(seq, c, sl):
        cps = []
        for p in range(PAGES_PER_CHUNK):
            page = pt_ref[seq, c * PAGES_PER_CHUNK + p]
            cps.append(pltpu.make_async_copy(ck_ref.at[page], kbuf.at[sl, p], ksem.at[sl]))
            cps.append(pltpu.make_async_copy(cv_ref.at[page], vbuf.at[sl, p], vsem.at[sl]))
        return cps

    @pl.when(b == 0)
    def _():
        for cp in chunk_copies(0, 0, 0):
            cp.start()

    hrow = lax.broadcasted_iota(jnp.int32, (A_HEADS, A_WIDTH), 0)
    hcol = lax.broadcasted_iota(jnp.int32, (A_HEADS, A_WIDTH), 1) // A_HEAD_DIM
    head_mask = jnp.where(hrow == hcol, 1.0, 0.0)
    qf = q_ref[0].astype(F32)
    qbd = jnp.concatenate([qf[t:t + 1, :] * head_mask for t in range(n_new)], axis=0).astype(BF16)
    nt = (((1,), (1,)), ((), ()))

    def softmax_step(state, logits, lanes, weighted_values):
        m_old, l_old, acc = state
        sel = sel_ref[0, :, lanes]
        sel_rows = jnp.concatenate([jnp.broadcast_to(sel[t:t + 1, :], (A_HEADS, sel.shape[1])) for t in range(n_new)],
                                   axis=0)
        x = jnp.where(sel_rows > 0.5, logits + bias_ref[:, lanes], NEG_BIG)
        m_new = jnp.maximum(m_old, jnp.max(x, axis=1, keepdims=True))
        alpha = jnp.exp(m_old - m_new)
        pm = jnp.exp(x - m_new)
        l_new = alpha * l_old + jnp.sum(pm, axis=1, keepdims=True)
        return m_new, l_new, alpha * acc + weighted_values(pm.astype(BF16))

    def chunk_body(c, state):
        sl = c % 2

        @pl.when(c + 1 < n_chunks)
        def _():
            for cp in chunk_copies(b, c + 1, 1 - sl):
                cp.start()

        @pl.when((c + 1 == n_chunks) & (b + 1 < pl.num_programs(0)))
        def _():
            for cp in chunk_copies(b + 1, 0, 1 - sl):
                cp.start()

        for cp in chunk_copies(b, c, sl):
            cp.wait()
        logits = jnp.concatenate([jnp.dot(qbd, kbuf[sl, p].astype(BF16), preferred_element_type=F32)
                                  for p in range(PAGES_PER_CHUNK)], axis=1)

        def weighted_values(pm):
            acc = jnp.zeros((rows, A_WIDTH), F32)
            for p in range(PAGES_PER_CHUNK):
                acc = acc + lax.dot_general(pm[:, p * PAGE_SIZE:(p + 1) * PAGE_SIZE], vbuf[sl, p].astype(BF16), nt,
                                            preferred_element_type=F32)
            return acc

        lanes = pl.ds(pl.multiple_of(c * keys_per_chunk, keys_per_chunk), keys_per_chunk)
        return softmax_step(state, logits, lanes, weighted_values)

    state = (jnp.full((rows, 1), NEG_BIG, F32), jnp.zeros((rows, 1), F32), jnp.zeros((rows, A_WIDTH), F32))
    state = lax.fori_loop(0, n_chunks, chunk_body, state)
    logits = lax.dot_general(qbd, kn_ref[0], nt, preferred_element_type=F32)
    _, l_fin, acc = softmax_step(state, logits, slice(n_pages * PAGE_SIZE, n_pages * PAGE_SIZE + NEW_PAD),
                                 lambda pm: jnp.dot(pm, vn_ref[0], preferred_element_type=F32))
    full_mask = jnp.concatenate([head_mask] * n_new, axis=0)
    o = (acc / l_fin) * full_mask
    o_ref[0] = jnp.sum(o.reshape(n_new, A_HEADS, A_WIDTH), axis=1).astype(o_ref.dtype)


def _sample_attend(page_table, q_bf, sel, bias_rows, k_new_pad, v_new_pad, cache_k_t, cache_v_t):
    nb, n_pages = page_table.shape
    n_new = q_bf.shape[1]
    assert n_pages % (2 * PAGES_PER_CHUNK) == 0
    assert cache_k_t.shape[1:] == (A_WIDTH, PAGE_SIZE)
    lk = n_pages * PAGE_SIZE + NEW_PAD
    page_buf = pltpu.VMEM((2, PAGES_PER_CHUNK, A_WIDTH, PAGE_SIZE), F32)
    grid_spec = pltpu.PrefetchScalarGridSpec(
        num_scalar_prefetch=1,
        grid=(nb,),
        in_specs=[pl.BlockSpec((1, n_new, A_WIDTH), lambda b, pt: (b, 0, 0)),
                  pl.BlockSpec((1, n_new, lk), lambda b, pt: (b, 0, 0)),
                  pl.BlockSpec((n_new * A_HEADS, lk), lambda b, pt: (0, 0), pipeline_mode=pl.Buffered(1)),
                  pl.BlockSpec((1, NEW_PAD, A_WIDTH), lambda b, pt: (b, 0, 0)),
                  pl.BlockSpec((1, NEW_PAD, A_WIDTH), lambda b, pt: (b, 0, 0)),
                  pl.BlockSpec(memory_space=pl.ANY),
                  pl.BlockSpec(memory_space=pl.ANY)],
        out_specs=pl.BlockSpec((1, n_new, A_WIDTH), lambda b, pt: (b, 0, 0)),
        scratch_shapes=[page_buf, page_buf, pltpu.SemaphoreType.DMA((2,)), pltpu.SemaphoreType.DMA((2,))],
    )
    return pl.pallas_call(
        functools.partial(_sample_attend_kernel, n_pages=n_pages, n_new=n_new),
        grid_spec=grid_spec,
        out_shape=jax.ShapeDtypeStruct((nb, n_new, A_WIDTH), BF16),
        compiler_params=_cparams(("arbitrary",)),
        name="sample_attend",
    )(page_table, q_bf, sel, bias_rows, k_new_pad, v_new_pad, cache_k_t, cache_v_t)


def _dsa_sample(qa_bf, kab, vab, qi_bf, kiw, rel_bias, cache_k, cache_v, cache_ki, page_table, n_new):
    nb, n_pages = page_table.shape
    n_past = n_pages * PAGE_SIZE
    assert n_new <= NEW_PAD and n_past >= REL_MAX_DIST
    topk = min(INDEX_TOPK, (n_past + n_new) // 4)
    qi_rows = qi_bf.reshape(nb, n_new, IDX_HEADS, IDX_DIM).transpose(0, 2, 1, 3).reshape(nb, IDX_HEADS * n_new, IDX_DIM)
    wi = kiw[:, IDX_DIM:IDX_DIM + IDX_HEADS] * INDEX_SCALE
    w_rows = jnp.broadcast_to(wi.reshape(nb, n_new, IDX_HEADS).transpose(0, 2, 1).reshape(nb, IDX_HEADS * n_new, 1),
                              (nb, IDX_HEADS * n_new, LANES))
    pad_rows = lambda a: jnp.pad(a, ((0, 0), (0, NEW_PAD - a.shape[1]), (0, 0)))
    ki_new = pad_rows(kiw[:, :IDX_DIM].astype(BF16).reshape(nb, n_new, IDX_DIM))
    n_pool = cache_k.shape[0]
    cache_ki_t = cache_ki.transpose(0, 2, 1)
    cache_k_t = cache_k.transpose(0, 2, 3, 1).reshape(n_pool, A_WIDTH, PAGE_SIZE)
    cache_v_t = cache_v.transpose(0, 2, 3, 1).reshape(n_pool, A_WIDTH, PAGE_SIZE)
    scores = _sample_scores(page_table, qi_rows, w_rows, ki_new, cache_ki_t)
    lk = scores.shape[-1]
    sel = _sample_select(scores.reshape(nb * n_new, lk), topk, n_past, n_new, rows=32)
    n_tail = REL_MAX_DIST + NEW_PAD
    table = _bias_by_distance(rel_bias, REL_MAX_DIST + n_tail)
    dist = (n_past + jnp.arange(n_new))[:, None] - (n_past - REL_MAX_DIST + jnp.arange(n_tail))[None, :]
    tail = table[:, jnp.clip(dist, 0, table.shape[1] - 1)].transpose(1, 0, 2)
    far = jnp.broadcast_to(table[None, :, -1:], (n_new, A_HEADS, n_past - REL_MAX_DIST))
    bias_rows = jnp.concatenate([far, tail], axis=2).reshape(n_new * A_HEADS, lk)
    o = _sample_attend(page_table, qa_bf.reshape(nb, n_new, A_WIDTH), sel.reshape(nb, n_new, lk), bias_rows,
                       pad_rows(kab.reshape(nb, n_new, A_WIDTH)), pad_rows(vab.reshape(nb, n_new, A_WIDTH)),
                       cache_k_t, cache_v_t)
    return o.reshape(nb * n_new, A_WIDTH)


def kernel(x_prompt, x_sample, cache_k, cache_v, cache_idx_k, state_gla, page_table, norm_mix, w_in, w_gate_up, b_gate, gla_norm, w_branch_a, w_branch_b, w_out, norm_ffn, w_router, b_router, w_exp_up, b_exp_up, w_exp_down, b_exp_down, rel_bias, norm_final):
    batch, seq, d = x_prompt.shape
    nb, n_new, _ = x_sample.shape
    assert w_in.shape[0] == 1, "the final norm is fused into the single layer's MoE kernel"
    layer = 0
    w_packed = _pack_w_in(w_in[layer])

    def trunk(x2d, attend, gla_args, tm_moe):
        tokens = x2d.shape[0]
        tm = min(512, tokens)
        qa, ka, kab, va, vab, qi, kiw, qb, kb, vb, r, glow, ga, gb = _inproj(x2d, norm_mix[layer], w_packed, tm)
        oa = attend(qa, kab, vab, qi, kiw)
        ob, state = gla_args(qb, kb, vb, r, glow)
        x2, h2, comb = _merge(oa, ob, ga, gb, x2d, w_branch_a[layer], w_branch_b[layer], w_out[layer], norm_ffn[layer],
                              w_router[layer], b_router[layer], tm)
        y = _moe(h2, comb, x2, w_exp_up[layer], b_exp_up[layer], w_exp_down[layer], b_exp_down[layer], norm_final,
                 min(tm_moe, tokens))
        return y, ka, va, kiw[:, :IDX_DIM], state

    gla_w = (w_gate_up[layer], b_gate[layer], gla_norm[layer])

    def gla_prompt(qb, kb, vb, r, glow):
        s0 = jnp.zeros((batch, B_HEADS, B_KEY_DIM, B_VAL_DIM), F32)
        return _gla(qb, kb, vb, r, glow, *gla_w, s0, batch, seq, GLA_CHUNK)

    def gla_sample(qb, kb, vb, r, glow):
        pad = lambda a: jnp.pad(a.reshape(nb, n_new, -1), ((0, 0), (0, GLA_CHUNK - n_new), (0, 0))).reshape(nb * GLA_CHUNK, -1)
        o, state = _gla(pad(qb), pad(kb), pad(vb), pad(r), pad(glow), *gla_w, state_gla[layer].astype(F32), nb, GLA_CHUNK,
                        n_new)
        return o.reshape(nb, GLA_CHUNK, -1)[:, :n_new].reshape(nb * n_new, -1), state

    yp, kp, vp, kip, sp = trunk(
        x_prompt.reshape(batch * seq, d),
        lambda qa, kab, vab, qi, kiw: _dsa_prompt(qa, kab, vab, qi, kiw, rel_bias, batch, seq),
        gla_prompt, 1024)
    ys, ks, vs, kis, ss = trunk(
        x_sample.reshape(nb * n_new, d),
        lambda qa, kab, vab, qi, kiw: _dsa_sample(qa, kab, vab, qi, kiw, rel_bias, cache_k[layer], cache_v[layer],
                                                  cache_idx_k[layer], page_table, n_new),
        gla_sample, 512)
    heads = (A_HEADS, A_HEAD_DIM)
    return (yp.reshape(batch, seq, d), ys.reshape(nb, n_new, d),
            kp.reshape(1, batch, seq, *heads), vp.reshape(1, batch, seq, *heads), kip.reshape(1, batch, seq, IDX_DIM),
            sp.astype(x_prompt.dtype)[None],
            ks.reshape(1, nb, n_new, *heads), vs.reshape(1, nb, n_new, *heads), kis.reshape(1, nb, n_new, IDX_DIM),
            ss.astype(state_gla.dtype)[None])
```

```python
import functools
import math

import numpy as np
import jax
import jax.numpy as jnp
from jax import lax
from jax.experimental import pallas as pl
from jax.experimental.pallas import tpu as pltpu

F32 = jnp.float32
BF16 = jnp.bfloat16

A_HEADS = 8
A_HEAD_DIM = 64
A_WIDTH = A_HEADS * A_HEAD_DIM
IDX_HEADS = 4
IDX_DIM = 64
INDEX_TOPK = 256
INDEX_SCALE = (IDX_HEADS * IDX_DIM) ** -0.5
REL_BUCKETS = 32
REL_MAX_DIST = 128
B_HEADS = 4
B_KEY_DIM = 64
B_VAL_DIM = 128
B_QK_WIDTH = B_HEADS * B_KEY_DIM
B_V_WIDTH = B_HEADS * B_VAL_DIM
GATE_RANK = 16
GATE_TAU = 16.0
GLA_CHUNK = 64
GLA_SUB = 16
TOP_K = 4
SWIGLU_ALPHA = 1.702
SWIGLU_LIMIT = 7.0
EPS = 1e-6
PAGE_SIZE = 128

LANES = 128
NEG_BIG = -1e30
INT_MIN = -2 ** 31
VMEM_LIMIT = 56 * 1024 * 1024


def _cparams(sem):
    return pltpu.CompilerParams(dimension_semantics=sem, vmem_limit_bytes=VMEM_LIMIT)


def _rms(xf, g):
    return xf * lax.rsqrt(jnp.mean(xf * xf, axis=-1, keepdims=True) + EPS) * g


_C_QA, _C_KA, _C_VA = 0, 512, 1024
_C_QI, _C_KIW = 1536, 1792
_C_QB, _C_KB, _C_VB, _C_R, _C_GLOW = 1920, 2176, 2432, 2944, 3456
_C_GA, _C_GB = 3584, 4608
_C_END = 5632


def _pack_w_in(w_in):
    d = w_in.shape[0]
    o = np.cumsum([0, 512, 512, 512, 256, 64, 4, 256, 256, 512, 16, 512, 1024, 1024]).tolist()
    seg = lambda i: w_in[:, o[i]:o[i + 1]]
    z = lambda n: jnp.zeros((d, n), w_in.dtype)
    packed = jnp.concatenate(
        [seg(0), seg(1), seg(2), seg(3), seg(4), seg(5), z(60),
         seg(6), seg(7), seg(8), seg(10), seg(9), z(112), seg(11), seg(12)], axis=1)
    assert packed.shape[1] == _C_END
    return packed.astype(BF16)


def _inproj_kernel(x_ref, g_ref, w_ref, qa_ref, ka_ref, kab_ref, va_ref, vab_ref, qi_ref, kiw_ref,
                   qb_ref, kb_ref, vb_ref, r_ref, glow_ref, ga_ref, gb_ref):
    h = _rms(x_ref[...], g_ref[...]).astype(BF16)

    def proj(a, b):
        return jnp.dot(h, w_ref[:, a:b], preferred_element_type=F32)

    za = proj(_C_QA, _C_QI)
    qa_ref[...] = (za[:, 0:512] * (A_HEAD_DIM ** -0.5)).astype(BF16)
    ka = za[:, 512:1024]
    va = za[:, 1024:1536]
    ka_ref[...] = ka
    kab_ref[...] = ka.astype(BF16)
    va_ref[...] = va
    vab_ref[...] = va.astype(BF16)
    zi = proj(_C_QI, _C_QB)
    qi_ref[...] = zi[:, 0:256].astype(BF16)
    kiw_ref[...] = zi[:, 256:384]
    zb = proj(_C_QB, _C_GA)
    qb_ref[...] = zb[:, 0:256]
    kb_ref[...] = zb[:, 256:512]
    vb_ref[...] = zb[:, 512:1024]
    r_ref[...] = zb[:, 1024:1536]
    glow_ref[...] = zb[:, 1536:1664]
    zg = proj(_C_GA, _C_END)
    ga_ref[...] = zg[:, 0:1024]
    gb_ref[...] = zg[:, 1024:2048]


def _inproj(x2d, g, w_packed, tm):
    t, d = x2d.shape
    widths = [(512, BF16), (512, F32), (512, BF16), (512, F32), (512, BF16), (256, BF16), (128, F32),
              (256, F32), (256, F32), (512, F32), (512, F32), (128, F32), (1024, F32), (1024, F32)]
    return pl.pallas_call(
        _inproj_kernel,
        grid=(t // tm,),
        in_specs=[pl.BlockSpec((tm, d), lambda i: (i, 0)),
                  pl.BlockSpec((1, d), lambda i: (0, 0)),
                  pl.BlockSpec((d, _C_END), lambda i: (0, 0))],
        out_specs=[pl.BlockSpec((tm, w), lambda i: (i, 0)) for w, _ in widths],
        out_shape=[jax.ShapeDtypeStruct((t, w), dt) for w, dt in widths],
        compiler_params=_cparams(("arbitrary",)),
        name="inproj",
    )(x2d, g.reshape(1, d), w_packed)


def _sortable_key(s):
    bits = pltpu.bitcast(s, jnp.int32)
    key = bits ^ ((bits >> 31) & 0x7FFFFFFF)
    return jnp.where(key == -1, 0, key)


def _t5_bucket_np(dist):
    max_exact = REL_BUCKETS // 2
    ratio = np.log(np.maximum(dist, 1).astype(np.float32) / np.float32(max_exact)) / np.float32(
        math.log(REL_MAX_DIST / max_exact))
    large = np.minimum(max_exact + (ratio * (REL_BUCKETS - max_exact)).astype(np.int32), REL_BUCKETS - 1)
    return np.where(dist < max_exact, dist, large).astype(np.int32)


def _bias_by_distance(rel_bias, n):
    dist = jnp.arange(n, dtype=jnp.int32)
    max_exact = REL_BUCKETS // 2
    log_ratio = jnp.log(jnp.maximum(dist, 1).astype(F32) / max_exact) / math.log(REL_MAX_DIST / max_exact)
    large = jnp.minimum(max_exact + (log_ratio * (REL_BUCKETS - max_exact)).astype(jnp.int32), REL_BUCKETS - 1)
    bucket = jnp.where(dist < max_exact, dist, large)
    return rel_bias[bucket].T.astype(F32)


def _dsa_prompt_kernel(rel_ref, qT_ref, qiT_ref, wT_ref, k_ref, vT_ref, ki_ref, bkt_ref, o_ref,
                       s_scr, qbd_scr, acc_scr, m_scr, l_scr, bias_scr, mask_scr, *, tq, tk, topk):
    i = pl.program_id(1)

    @pl.when((pl.program_id(0) == 0) & (i == 0))
    def _():
        for which in range(2 * (tk // tq)):
            bkt = bkt_ref[which]

            def one_head(h, carry):
                far = rel_ref[REL_BUCKETS - 1, h]
                bias_scr[which, h] = lax.fori_loop(
                    0, REL_BUCKETS - 1, lambda b, t: jnp.where(bkt == b, rel_ref[b, h] - far, t),
                    jnp.zeros((tk, tq), F32))
                return carry

            lax.fori_loop(0, A_HEADS, one_head, 0)

    diag = (i * tq) // tk
    nkb = diag + 1
    qpos = i * tq + lax.broadcasted_iota(jnp.int32, (1, tq), 1)
    krow = lax.broadcasted_iota(jnp.int32, (tk, 1), 0)

    qiT = qiT_ref[0]
    qi_st = jnp.concatenate([qiT[h * IDX_DIM:(h + 1) * IDX_DIM, :] for h in range(IDX_HEADS)], axis=1)
    w = wT_ref[0]

    def score_block(j, carry):
        s = jnp.dot(ki_ref[0, j], qi_st, preferred_element_type=F32)
        acc = jnp.maximum(s[:, 0:tq], 0.0) * w[0:1, :]
        for h in range(1, IDX_HEADS):
            acc = acc + jnp.maximum(s[:, h * tq:(h + 1) * tq], 0.0) * w[h:h + 1, :]
        acc = jnp.where(j * tk + krow <= qpos, acc, -jnp.inf)
        s_scr[pl.ds(pl.multiple_of(j * tk, tk), tk), :] = _sortable_key(acc)
        return carry

    lax.fori_loop(0, nkb, score_block, 0)

    def count_ge(cand):
        def body(j, c):
            blk = s_scr[pl.ds(pl.multiple_of(j * tk, tk), tk), :]
            hit = jnp.where(blk >= cand, 1, 0).astype(jnp.int32)
            return c + jnp.sum(hit.reshape(tk // 8, 8, tq), axis=0)
        c = lax.fori_loop(0, nkb, body, jnp.zeros((8, tq), jnp.int32))
        return jnp.sum(c, axis=0, keepdims=True)

    def bisect(step, thr):
        cand = thr + lax.shift_left(jnp.int32(1), 31 - step)
        return jnp.where(count_ge(cand) >= topk, cand, thr)

    thr = lax.fori_loop(0, 32, bisect, jnp.full((1, tq), INT_MIN, jnp.int32))
    n_take = (topk - count_ge(thr + 1)).astype(F32)

    rowid = lax.broadcasted_iota(jnp.int32, (2 * A_HEAD_DIM, tq), 0)
    for p in range(A_HEADS // 2):
        slab = qT_ref[0, p * 128:(p + 1) * 128, :].astype(F32)
        top = jnp.where(rowid < A_HEAD_DIM, slab, 0.0)
        qbd_scr[p] = jnp.concatenate([top, slab - top], axis=1).astype(BF16)

    acc_scr[...] = jnp.zeros_like(acc_scr)
    m_scr[...] = jnp.full_like(m_scr, NEG_BIG)
    l_scr[...] = jnp.zeros_like(l_scr)
    ri = lax.broadcasted_iota(jnp.int32, (tk, tk), 0)
    ci = lax.broadcasted_iota(jnp.int32, (tk, tk), 1)
    ltri = jnp.where(ci < ri, 1.0, 0.0).astype(BF16)

    def attend_block(j, tie_seen, near):
        keyblk = s_scr[pl.ds(pl.multiple_of(j * tk, tk), tk), :]
        eq = jnp.where(keyblk == thr, 1.0, 0.0)
        before = tie_seen + jnp.dot(ltri, eq.astype(BF16), preferred_element_type=F32)
        take = jnp.where(keyblk > thr, 1.0, jnp.where(before < n_take, eq, 0.0))
        if near:
            take = jnp.where(j * tk + krow <= qpos, take, 0.0)
            which = 2 * ((i * tq) % tk // tq) + (diag - j)
        mask_scr[...] = jnp.where(take > 0.5, 0.0, NEG_BIG)
        for p in range(A_HEADS // 2):
            lg = jnp.dot(k_ref[0, j, :, p * 128:(p + 1) * 128], qbd_scr[p], preferred_element_type=F32)
            for u in range(2):
                h = 2 * p + u
                x = lg[:, u * tq:(u + 1) * tq] + mask_scr[...]
                if near:
                    x = x + bias_scr[which, h]
                m_old = m_scr[h:h + 1, :]
                m_new = jnp.maximum(m_old, jnp.max(x, axis=0, keepdims=True))
                alpha = jnp.exp(m_old - m_new)
                pm = jnp.exp(x - m_new)
                l_scr[h:h + 1, :] = alpha * l_scr[h:h + 1, :] + jnp.sum(pm, axis=0, keepdims=True)
                pv = jnp.dot(vT_ref[0, j, h * A_HEAD_DIM:(h + 1) * A_HEAD_DIM, :], pm.astype(BF16),
                             preferred_element_type=F32)
                rows = slice(h * A_HEAD_DIM, (h + 1) * A_HEAD_DIM)
                acc_scr[rows, :] = alpha * acc_scr[rows, :] + pv
                m_scr[h:h + 1, :] = m_new
        return tie_seen + jnp.sum(eq, axis=0, keepdims=True)

    n_far = jnp.maximum(diag - 1, 0)
    tie_seen = lax.fori_loop(0, n_far, functools.partial(attend_block, near=False), jnp.zeros((1, tq), F32))
    lax.fori_loop(n_far, nkb, functools.partial(attend_block, near=True), tie_seen)

    for h in range(A_HEADS):
        rows = slice(h * A_HEAD_DIM, (h + 1) * A_HEAD_DIM)
        acc_scr[rows, :] = acc_scr[rows, :] / l_scr[h:h + 1, :]
    o_ref[0] = acc_scr[...].T.astype(o_ref.dtype)


def _dsa_prompt(qa_bf, ka_bf, va_bf, qi_bf, kiw, rel_bias, batch, seq):
    tq, tk = 256, 256
    assert seq % tk == 0 and tk % tq == 0
    topk = min(INDEX_TOPK, seq // 4)
    nk = seq // tk
    qT = qa_bf.reshape(batch, seq, A_WIDTH).transpose(0, 2, 1)
    qiT = qi_bf.reshape(batch, seq, IDX_HEADS * IDX_DIM).transpose(0, 2, 1)
    wi = kiw[:, IDX_DIM:IDX_DIM + IDX_HEADS] * INDEX_SCALE
    wT = jnp.pad(wi.reshape(batch, seq, IDX_HEADS).transpose(0, 2, 1), ((0, 0), (0, 8 - IDX_HEADS), (0, 0)))
    k4 = ka_bf.reshape(batch, nk, tk, A_WIDTH)
    vT4 = va_bf.reshape(batch, nk, tk, A_WIDTH).transpose(0, 1, 3, 2)
    ki4 = kiw[:, :IDX_DIM].astype(BF16).reshape(batch, nk, tk, IDX_DIM)
    assert REL_MAX_DIST <= tk
    d0 = np.arange(tq)[None, :] - np.arange(tk)[:, None]
    bkt = jnp.asarray(np.stack([_t5_bucket_np(np.maximum(d0 + off * tq + kind * tk, 0))
                                for off in range(tk // tq) for kind in range(2)]), jnp.int32)

    once = dict(pipeline_mode=pl.Buffered(1))
    kern = functools.partial(_dsa_prompt_kernel, tq=tq, tk=tk, topk=topk)
    out = pl.pallas_call(
        kern,
        grid=(batch, seq // tq),
        in_specs=[pl.BlockSpec(memory_space=pltpu.SMEM),
                  pl.BlockSpec((1, A_WIDTH, tq), lambda b, i: (b, 0, i)),
                  pl.BlockSpec((1, IDX_HEADS * IDX_DIM, tq), lambda b, i: (b, 0, i)),
                  pl.BlockSpec((1, 8, tq), lambda b, i: (b, 0, i)),
                  pl.BlockSpec((1, nk, tk, A_WIDTH), lambda b, i: (b, 0, 0, 0), **once),
                  pl.BlockSpec((1, nk, A_WIDTH, tk), lambda b, i: (b, 0, 0, 0), **once),
                  pl.BlockSpec((1, nk, tk, IDX_DIM), lambda b, i: (b, 0, 0, 0), **once),
                  pl.BlockSpec(bkt.shape, lambda b, i: (0, 0, 0), **once)],
        out_specs=pl.BlockSpec((1, tq, A_WIDTH), lambda b, i: (b, i, 0)),
        out_shape=jax.ShapeDtypeStruct((batch, seq, A_WIDTH), BF16),
        scratch_shapes=[pltpu.VMEM((seq, tq), jnp.int32),
                        pltpu.VMEM((A_HEADS // 2, 2 * A_HEAD_DIM, 2 * tq), BF16),
                        pltpu.VMEM((A_WIDTH, tq), F32),
                        pltpu.VMEM((A_HEADS, tq), F32),
                        pltpu.VMEM((A_HEADS, tq), F32),
                        pltpu.VMEM((bkt.shape[0], A_HEADS, tk, tq), F32),
                        pltpu.VMEM((tk, tq), F32)],
        compiler_params=_cparams(("arbitrary", "arbitrary")),
        name="dsa_prompt",
    )(rel_bias.astype(F32), qT, qiT, wT, k4, vT4, ki4, bkt)
    return out.reshape(batch * seq, A_WIDTH)


def _gla_kernel(q_ref, k_ref, v_ref, r_ref, glow_ref, wg_ref, bg_ref, gn_ref, s0_ref, o_ref, s_out_ref,
                sT_ref, q_scr, k_scr, b_scr, att_scr, *, n_valid):
    C = GLA_CHUNK
    hp = lax.Precision.HIGHEST

    @pl.when(pl.program_id(1) == 0)
    def _():
        for p in range(B_HEADS // 2):
            sT_ref[p] = s0_ref[0, 2 * p:2 * p + 2].reshape(LANES, B_VAL_DIM).T

    g = jnp.dot(glow_ref[0], wg_ref[...], precision=hp, preferred_element_type=F32) + bg_ref[...]
    log_a = (jnp.minimum(g, 0.0) - jnp.log1p(jnp.exp(-jnp.abs(g)))) * (1.0 / GATE_TAU)
    if n_valid < C:
        log_a = jnp.where(lax.broadcasted_iota(jnp.int32, (C, 1), 0) < n_valid, log_a, 0.0)
        att_scr[...] = jnp.zeros_like(att_scr)
    tri = jnp.where(lax.broadcasted_iota(jnp.int32, (C, C), 1) <= lax.broadcasted_iota(jnp.int32, (C, C), 0), 1.0, 0.0)
    b = jnp.dot(tri, log_a, precision=hp, preferred_element_type=F32)
    q = q_ref[0] * (B_KEY_DIM ** -0.5)
    k = k_ref[0]
    v = v_ref[0]
    q_scr[...] = q
    k_scr[...] = k
    b_scr[...] = b
    eb = jnp.exp(b)
    qe = q * eb
    kdec = k * jnp.exp(b[C - 1:C, :] - b)
    eb_last = eb[C - 1:C, :]

    lane1 = lax.broadcasted_iota(jnp.int32, (1, LANES), 1)
    head_mask = (jnp.where(lane1 < B_KEY_DIM, 1.0, 0.0), jnp.where(lane1 >= B_KEY_DIM, 1.0, 0.0))
    lane_c = lax.broadcasted_iota(jnp.int32, (C, LANES), 1) % B_KEY_DIM
    row_c = lax.broadcasted_iota(jnp.int32, (C, LANES), 0)
    causal = jnp.where(lane_c <= row_c, 1.0, 0.0)
    lane_s = lax.broadcasted_iota(jnp.int32, (GLA_SUB, LANES), 1) % B_KEY_DIM
    row_s = lax.broadcasted_iota(jnp.int32, (GLA_SUB, LANES), 0)
    same_head = jnp.where(lax.broadcasted_iota(jnp.int32, (LANES, LANES), 0) // B_KEY_DIM
                          == lax.broadcasted_iota(jnp.int32, (LANES, LANES), 1) // B_KEY_DIM, 1.0, 0.0).astype(BF16)
    tn = (((0,), (0,)), ((), ()))
    nt = (((1,), (1,)), ((), ()))

    for p in range(B_HEADS // 2):
        sl = slice(p * LANES, (p + 1) * LANES)
        kp = k[:, sl]
        bp = b[:, sl]
        qp = q[:, sl]

        off_rows = [jnp.zeros((GLA_SUB, LANES), F32)]
        for big_i in range(1, C // GLA_SUB):
            lo_row = big_i * GLA_SUB
            if lo_row >= n_valid:
                off_rows.append(jnp.zeros((GLA_SUB, LANES), F32))
                continue
            rho = b_scr[lo_row - 1:lo_row, sl]
            qt = qp[lo_row:lo_row + GLA_SUB] * jnp.exp(bp[lo_row:lo_row + GLA_SUB] - rho)
            kt = kp * jnp.exp(jnp.minimum(rho - bp, 0.0))
            kt2 = jnp.concatenate([kt * head_mask[0], kt * head_mask[1]], axis=0)
            qh = qt.astype(BF16)
            ql = (qt - qh.astype(F32)).astype(BF16)
            kh = kt2.astype(BF16)
            kl = (kt2 - kh.astype(F32)).astype(BF16)
            a = (lax.dot_general(qh, kh, nt, preferred_element_type=F32)
                 + lax.dot_general(qh, kl, nt, preferred_element_type=F32)
                 + lax.dot_general(ql, kh, nt, preferred_element_type=F32))
            off_rows.append(jnp.where(lane_s < lo_row, a, 0.0))
        att_off = jnp.concatenate(off_rows, axis=0)

        def intra_rows(grp, carry):
            base = pl.multiple_of(grp * 8, 8)
            sub0 = pl.multiple_of(grp // (GLA_SUB // 8) * GLA_SUB, GLA_SUB)
            b8 = b_scr[pl.ds(base, 8), sl]
            q8 = q_scr[pl.ds(base, 8), sl]
            ksub = k_scr[pl.ds(sub0, GLA_SUB), sl]
            bsub = b_scr[pl.ds(sub0, GLA_SUB), sl]
            on_diag = jnp.where(lane_s == sub0 + row_s, 1.0, 0.0)
            xs = []
            for rr in range(8):
                if rr >= n_valid:
                    xs.append(jnp.zeros((GLA_SUB, LANES), F32))
                else:
                    xs.append((q8[rr:rr + 1, :] * ksub) * jnp.exp(jnp.minimum(b8[rr:rr + 1, :] - bsub, 0.0)))
            x = jnp.concatenate(xs, axis=0)
            xh = x.astype(BF16)
            xl = (x - xh.astype(F32)).astype(BF16)
            seg = (jnp.dot(xh, same_head, preferred_element_type=F32)
                   + jnp.dot(xl, same_head, preferred_element_type=F32))
            att_scr[pl.ds(base, 8), sl] = jnp.concatenate(
                [jnp.sum(seg[rr * GLA_SUB:(rr + 1) * GLA_SUB] * on_diag, axis=0, keepdims=True) for rr in range(8)], axis=0)
            return carry

        lax.fori_loop(0, (n_valid + 7) // 8, intra_rows, 0)
        att = (att_scr[:, sl] + att_off) * causal
        sT = sT_ref[p]
        sT_bf = sT.astype(BF16)
        v_pair = jnp.concatenate([v[:, (2 * p) * B_VAL_DIM:(2 * p + 1) * B_VAL_DIM],
                                  v[:, (2 * p + 1) * B_VAL_DIM:(2 * p + 2) * B_VAL_DIM]], axis=0).astype(BF16)
        upd = jnp.zeros((B_VAL_DIM, LANES), F32)
        for u in range(2):
            h = 2 * p + u
            hs = slice(h * B_VAL_DIM, (h + 1) * B_VAL_DIM)
            o = lax.dot_general((qe[:, sl] * head_mask[u]).astype(BF16), sT_bf, nt, preferred_element_type=F32)
            o = o + jnp.dot((att * head_mask[u]).astype(BF16), v_pair, preferred_element_type=F32)
            o = o * lax.rsqrt(jnp.mean(o * o, axis=-1, keepdims=True) + EPS)
            rh = r_ref[0, :, hs]
            o_ref[0, :, hs] = (o * gn_ref[:, hs] * (rh * jax.nn.sigmoid(rh))).astype(o_ref.dtype)
            upd = upd + lax.dot_general(v[:, hs].astype(BF16), (kdec[:, sl] * head_mask[u]).astype(BF16), tn,
                                        preferred_element_type=F32)
        s_new = sT * eb_last[:, sl] + upd
        sT_ref[p] = s_new

        @pl.when(pl.program_id(1) == pl.num_programs(1) - 1)
        def _():
            s_out_ref[0, 2 * p:2 * p + 2] = s_new.T.reshape(2, B_KEY_DIM, B_VAL_DIM)


def _gla(qb, kb, vb, r, glow, w_gate_up, b_gate, gla_norm, s0, nb, length, n_valid):
    C = GLA_CHUNK
    assert length % C == 0 and B_KEY_DIM * 2 == LANES and C == B_KEY_DIM and B_VAL_DIM == LANES
    rs = lambda a: a.reshape(nb, length, a.shape[-1])
    wg = jnp.pad(w_gate_up, ((0, LANES - GATE_RANK), (0, 0)))
    tok = lambda w: pl.BlockSpec((1, C, w), lambda b, c: (b, c, 0))
    full = lambda shape: pl.BlockSpec(shape, lambda b, c: (0,) * len(shape))
    st = pl.BlockSpec((1, B_HEADS, B_KEY_DIM, B_VAL_DIM), lambda b, c: (b, 0, 0, 0))
    o, state = pl.pallas_call(
        functools.partial(_gla_kernel, n_valid=n_valid),
        grid=(nb, length // C),
        in_specs=[tok(B_QK_WIDTH), tok(B_QK_WIDTH), tok(B_V_WIDTH), tok(B_V_WIDTH), tok(LANES),
                  full((LANES, B_QK_WIDTH)), full((1, B_QK_WIDTH)), full((1, B_V_WIDTH)), st],
        out_specs=[tok(B_V_WIDTH), st],
        out_shape=[jax.ShapeDtypeStruct((nb, length, B_V_WIDTH), BF16),
                   jax.ShapeDtypeStruct((nb, B_HEADS, B_KEY_DIM, B_VAL_DIM), F32)],
        scratch_shapes=[pltpu.VMEM((B_HEADS // 2, B_VAL_DIM, LANES), F32)] + [pltpu.VMEM((C, B_QK_WIDTH), F32)] * 4,
        compiler_params=_cparams(("arbitrary", "arbitrary")),
        name="gla",
    )(rs(qb), rs(kb), rs(vb), rs(r), rs(glow), wg, b_gate.reshape(1, -1), gla_norm.reshape(1, -1), s0)
    return o.reshape(nb * length, B_V_WIDTH), state


def _merge_kernel(oa_ref, ob_ref, ga_ref, gb_ref, x_ref, wa_ref, wb_ref, wo_ref, gf_ref, wr_ref, br_ref,
                  x2_ref, h2_ref, comb_ref, *, n_experts):
    ya = jnp.dot(oa_ref[...], wa_ref[...], preferred_element_type=F32)
    yb = jnp.dot(ob_ref[...], wb_ref[...], preferred_element_type=F32)
    merged = jax.nn.sigmoid(ga_ref[...]) * ya + jax.nn.sigmoid(gb_ref[...]) * yb
    x2 = x_ref[...] + jnp.dot(merged.astype(BF16), wo_ref[...], preferred_element_type=F32)
    x2_ref[...] = x2
    h2 = _rms(x2, gf_ref[...])
    h2_ref[...] = h2.T.astype(BF16)
    logits = jnp.dot(h2, wr_ref[...], precision=lax.Precision.HIGHEST, preferred_element_type=F32) + br_ref[...]
    lane = lax.broadcasted_iota(jnp.int32, logits.shape, 1).astype(F32)
    work = jnp.where(lane < n_experts, logits, -jnp.inf)
    picks = []
    for _ in range(TOP_K):
        m = jnp.max(work, axis=-1, keepdims=True)
        first = jnp.min(jnp.where(work == m, lane, float(LANES)), axis=-1, keepdims=True)
        hit = lane == first
        picks.append((m, hit))
        work = jnp.where(hit, -jnp.inf, work)
    es = [jnp.exp(m - picks[0][0]) for m, _ in picks]
    inv = 1.0 / (es[0] + es[1] + es[2] + es[3])
    comb = jnp.zeros_like(logits)
    for e, (_, hit) in zip(es, picks):
        comb = jnp.where(hit, e * inv, comb)
    comb_ref[...] = comb.T


def _merge(oa, ob, ga, gb, x2d, w_branch_a, w_branch_b, w_out, norm_ffn, w_router, b_router, tm):
    t, d = x2d.shape
    n_experts = w_router.shape[1]
    assert n_experts <= LANES and TOP_K == 4
    wr = jnp.pad(w_router, ((0, 0), (0, LANES - n_experts)))
    br = jnp.pad(b_router, (0, LANES - n_experts)).reshape(1, LANES)
    tok = lambda w: pl.BlockSpec((tm, w), lambda i: (i, 0))
    full = lambda a: pl.BlockSpec(a.shape, lambda i: (0,) * a.ndim)
    wa, wb, wo, gf = w_branch_a.astype(BF16), w_branch_b.astype(BF16), w_out.astype(BF16), norm_ffn.reshape(1, d)
    return pl.pallas_call(
        functools.partial(_merge_kernel, n_experts=n_experts),
        grid=(t // tm,),
        in_specs=[tok(A_WIDTH), tok(B_V_WIDTH), tok(d), tok(d), tok(d),
                  full(wa), full(wb), full(wo), full(gf), full(wr), full(br)],
        out_specs=[tok(d), pl.BlockSpec((d, tm), lambda i: (0, i)), pl.BlockSpec((LANES, tm), lambda i: (0, i))],
        out_shape=[jax.ShapeDtypeStruct((t, d), F32), jax.ShapeDtypeStruct((d, t), BF16),
                   jax.ShapeDtypeStruct((LANES, t), F32)],
        compiler_params=_cparams(("arbitrary",)),
        name="merge",
    )(oa, ob, ga, gb, x2d, wa, wb, wo, gf, wr, br)


MOE_CAP = 256


def _moe_kernel(hT_ref, combT_ref, x2_ref, wuT_ref, bu_ref, wdT_ref, bd_ref, gfin_ref, o_ref,
                outT_scr, rank_scr, comb_scr, *, n_experts):
    e = pl.program_id(1)
    d, tm = hT_ref.shape
    de = wdT_ref.shape[2]

    @pl.when(e == 0)
    def _():
        outT_scr[...] = jnp.zeros_like(outT_scr)
        ra = lax.broadcasted_iota(jnp.int32, (LANES, LANES), 0)
        ca = lax.broadcasted_iota(jnp.int32, (LANES, LANES), 1)
        upper = jnp.where(ra < ca, 1.0, 0.0).astype(BF16)
        seen = jnp.zeros((LANES, 1), F32)
        for c in range(tm // LANES):
            sl = slice(c * LANES, (c + 1) * LANES)
            comb_c = combT_ref[:, sl]
            routed = jnp.where(comb_c > 0.0, 1.0, 0.0)
            rank = seen + jnp.dot(routed.astype(BF16), upper, preferred_element_type=F32)
            rank = jnp.where(routed > 0.5, rank, -1.0)
            for x in range(n_experts):
                rank_scr[x, :, sl] = jnp.broadcast_to(rank[x:x + 1, :], (8, LANES))
                comb_scr[x, :, sl] = jnp.broadcast_to(comb_c[x:x + 1, :], (8, LANES))
            seen = seen + jnp.sum(routed, axis=1, keepdims=True)

    rank_rows = rank_scr[e]
    rank_row = rank_rows[0:1, :]
    comb_row = comb_scr[e][0:1, :]
    reps = MOE_CAP // LANES
    count = (jnp.sum(jnp.where(rank_rows >= 0.0, 1.0, 0.0)) * 0.125).astype(jnp.int32)
    slot = lax.broadcasted_iota(jnp.int32, (MOE_CAP, tm), 0).astype(F32)
    nt = (((1,), (1,)), ((), ()))

    def one_pass(n, carry):
        onehot = jnp.where(rank_row - (n * MOE_CAP).astype(F32) == slot, 1.0, 0.0).astype(BF16)
        xs = lax.dot_general(hT_ref[...], onehot, nt, preferred_element_type=F32).astype(BF16)
        z = jnp.dot(wuT_ref[0], xs, preferred_element_type=F32) + jnp.tile(bu_ref[0], (1, reps))
        glu = jnp.minimum(z[:de], SWIGLU_LIMIT)
        lin = jnp.clip(z[de:], -SWIGLU_LIMIT, SWIGLU_LIMIT)
        act = glu * jax.nn.sigmoid(SWIGLU_ALPHA * glu) * (lin + 1.0)
        y = jnp.dot(wdT_ref[0], act.astype(BF16), preferred_element_type=F32) + jnp.tile(bd_ref[0], (1, reps))
        outT_scr[...] += jnp.dot(y.astype(BF16), onehot, preferred_element_type=F32) * comb_row
        return carry

    lax.fori_loop(0, (count + MOE_CAP - 1) // MOE_CAP, one_pass, 0)

    @pl.when(e == pl.num_programs(1) - 1)
    def _():
        o_ref[...] = _rms(x2_ref[...] + outT_scr[...].T, gfin_ref[...])


def _moe(h2T, combT, x2, w_up, b_up, w_down, b_down, norm_final, tm):
    t, d = x2.shape
    n_experts, _, two_de = w_up.shape
    de = two_de // 2
    wuT = jnp.swapaxes(w_up, 1, 2).astype(BF16)
    wdT = jnp.swapaxes(w_down, 1, 2).astype(BF16)
    lane_copies = lambda bias: jnp.broadcast_to(bias[:, :, None], (*bias.shape, LANES))
    assert MOE_CAP % LANES == 0 and tm % LANES == 0
    return pl.pallas_call(
        functools.partial(_moe_kernel, n_experts=n_experts),
        grid=(t // tm, n_experts),
        in_specs=[pl.BlockSpec((d, tm), lambda i, e: (0, i)),
                  pl.BlockSpec((LANES, tm), lambda i, e: (0, i)),
                  pl.BlockSpec((tm, d), lambda i, e: (i, 0)),
                  pl.BlockSpec((1, two_de, d), lambda i, e: (e, 0, 0)),
                  pl.BlockSpec((1, two_de, LANES), lambda i, e: (e, 0, 0)),
                  pl.BlockSpec((1, d, de), lambda i, e: (e, 0, 0)),
                  pl.BlockSpec((1, d, LANES), lambda i, e: (e, 0, 0)),
                  pl.BlockSpec((1, d), lambda i, e: (0, 0))],
        out_specs=pl.BlockSpec((tm, d), lambda i, e: (i, 0)),
        out_shape=jax.ShapeDtypeStruct((t, d), F32),
        scratch_shapes=[pltpu.VMEM((d, tm), F32), pltpu.VMEM((n_experts, 8, tm), F32),
                        pltpu.VMEM((n_experts, 8, tm), F32)],
        compiler_params=_cparams(("arbitrary", "arbitrary")),
        name="moe",
    )(h2T, combT, x2, wuT, lane_copies(b_up), wdT, lane_copies(b_down), norm_final.reshape(1, d))


NEW_PAD = LANES
PAGES_PER_CHUNK = 8


def _sample_scores_kernel(pt_ref, qi_ref, w_ref, kin_ref, cki_ref, o_ref, buf, sem, *, n_pages, n_new):
    b = pl.program_id(0)
    slot = b % 2

    def page_copy(seq, p, sl):
        return pltpu.make_async_copy(cki_ref.at[pt_ref[seq, p]], buf.at[sl, p], sem.at[sl])

    @pl.when(b == 0)
    def _():
        for p in range(n_pages):
            page_copy(0, p, 0).start()

    @pl.when(b + 1 < pl.num_programs(0))
    def _():
        for p in range(n_pages):
            page_copy(b + 1, p, 1 - slot).start()

    for p in range(n_pages):
        page_copy(b, p, slot).wait()

    nt = (((1,), (1,)), ((), ()))
    qi = qi_ref[0]
    w = w_ref[0]
    half = IDX_HEADS * n_new // 2

    def reduce_heads(s, reps):
        wa = jnp.tile(w[0:half], (1, reps))
        wb = jnp.tile(w[half:2 * half], (1, reps))
        y = jnp.maximum(s[0:half], 0.0) * wa + jnp.maximum(s[half:2 * half], 0.0) * wb
        return (y + pltpu.roll(y, n_new, 0))[0:n_new]

    for c in range(n_pages // PAGES_PER_CHUNK):
        s = jnp.concatenate([jnp.dot(qi, buf[slot, c * PAGES_PER_CHUNK + p].astype(BF16), preferred_element_type=F32)
                             for p in range(PAGES_PER_CHUNK)], axis=1)
        o_ref[0, :, c * PAGES_PER_CHUNK * PAGE_SIZE:(c + 1) * PAGES_PER_CHUNK * PAGE_SIZE] = reduce_heads(
            s, PAGES_PER_CHUNK)
    s = lax.dot_general(qi, kin_ref[0], nt, preferred_element_type=F32)
    y = reduce_heads(s, 1)
    j = lax.broadcasted_iota(jnp.int32, (n_new, NEW_PAD), 1)
    t = lax.broadcasted_iota(jnp.int32, (n_new, NEW_PAD), 0)
    o_ref[0, :, n_pages * PAGE_SIZE:] = jnp.where(j <= t, y, -jnp.inf)


def _sample_scores(page_table, qi_rows, w_rows, ki_new_pad, cache_ki_t):
    nb, n_pages = page_table.shape
    n_new = qi_rows.shape[1] // IDX_HEADS
    assert n_new == 4 and n_pages % PAGES_PER_CHUNK == 0 and cache_ki_t.shape[1:] == (IDX_DIM, PAGE_SIZE)
    lk = n_pages * PAGE_SIZE + NEW_PAD
    grid_spec = pltpu.PrefetchScalarGridSpec(
        num_scalar_prefetch=1,
        grid=(nb,),
        in_specs=[pl.BlockSpec((1, IDX_HEADS * n_new, IDX_DIM), lambda b, pt: (b, 0, 0)),
                  pl.BlockSpec((1, IDX_HEADS * n_new, LANES), lambda b, pt: (b, 0, 0)),
                  pl.BlockSpec((1, NEW_PAD, IDX_DIM), lambda b, pt: (b, 0, 0)),
                  pl.BlockSpec(memory_space=pl.ANY)],
        out_specs=pl.BlockSpec((1, n_new, lk), lambda b, pt: (b, 0, 0)),
        scratch_shapes=[pltpu.VMEM((2, n_pages, IDX_DIM, PAGE_SIZE), F32), pltpu.SemaphoreType.DMA((2,))],
    )
    return pl.pallas_call(
        functools.partial(_sample_scores_kernel, n_pages=n_pages, n_new=n_new),
        grid_spec=grid_spec,
        out_shape=jax.ShapeDtypeStruct((nb, n_new, lk), F32),
        compiler_params=_cparams(("arbitrary",)),
        name="sample_scores",
    )(page_table, qi_rows, w_rows, ki_new_pad, cache_ki_t)


def _sample_select_kernel(s_ref, sel_ref, *, topk, n_past, n_new):
    rows, lk = s_ref.shape
    key = _sortable_key(s_ref[...])

    def count_ge(cand):
        return jnp.sum(jnp.where(key >= cand, 1.0, 0.0), axis=1, keepdims=True)

    def bisect(step, thr):
        cand = thr + lax.shift_left(jnp.int32(1), 31 - step)
        return jnp.where(count_ge(cand) >= topk, cand, thr)

    thr = lax.fori_loop(0, 32, bisect, jnp.full((rows, 1), INT_MIN, jnp.int32))
    n_take = topk - count_ge(thr + 1)
    col = lax.broadcasted_iota(jnp.int32, (rows, lk), 1)
    t = lax.broadcasted_iota(jnp.int32, (rows, lk), 0) % n_new
    visible = col - n_past <= t
    eq = jnp.where(key == thr, 1.0, 0.0)
    ra = lax.broadcasted_iota(jnp.int32, (LANES, LANES), 0)
    ca = lax.broadcasted_iota(jnp.int32, (LANES, LANES), 1)
    upper = jnp.where(ra < ca, 1.0, 0.0).astype(BF16)
    seen = jnp.zeros((rows, 1), F32)
    for c in range(lk // LANES):
        sl = slice(c * LANES, (c + 1) * LANES)
        eqc = eq[:, sl]
        before = seen + jnp.dot(eqc.astype(BF16), upper, preferred_element_type=F32)
        take = jnp.where(key[:, sl] > thr, 1.0, jnp.where(before < n_take, eqc, 0.0))
        sel_ref[:, sl] = jnp.where(visible[:, sl], take, 0.0)
        seen = seen + jnp.sum(eqc, axis=1, keepdims=True)


def _sample_select(scores2d, topk, n_past, n_new, rows):
    n, lk = scores2d.shape
    assert n % rows == 0 and rows % n_new == 0 and lk == n_past + NEW_PAD
    return pl.pallas_call(
        functools.partial(_sample_select_kernel, topk=topk, n_past=n_past, n_new=n_new),
        grid=(n // rows,),
        in_specs=[pl.BlockSpec((rows, lk), lambda i: (i, 0))],
        out_specs=pl.BlockSpec((rows, lk), lambda i: (i, 0)),
        out_shape=jax.ShapeDtypeStruct((n, lk), F32),
        compiler_params=_cparams(("arbitrary",)),
        name="sample_select",
    )(scores2d)


def _sample_attend_kernel(pt_ref, q_ref, sel_ref, bias_ref, kn_ref, vn_ref, ck_ref, cv_ref, o_ref,
                          kbuf, vbuf, ksem, vsem, *, n_pages, n_new):
    b = pl.program_id(0)
    n_chunks = n_pages // PAGES_PER_CHUNK
    keys_per_chunk = PAGES_PER_CHUNK * PAGE_SIZE
    rows = n_new * A_HEADS

    def chunk_copies(seq, c, sl):
        cps = []
        for p in range(PAGES_PER_CHUNK):
            page = pt_ref[seq, c * PAGES_PER_CHUNK + p]
            cps.append(pltpu.make_async_copy(ck_ref.at[page], kbuf.at[sl, p], ksem.at[sl]))
            cps.append(pltpu.make_async_copy(cv_ref.at[page], vbuf.at[sl, p], vsem.at[sl]))
        return cps

    @pl.when(b == 0)
    def _():
        for cp in chunk_copies(0, 0, 0):
            cp.start()

    hrow = lax.broadcasted_iota(jnp.int32, (A_HEADS, A_WIDTH), 0)
    hcol = lax.broadcasted_iota(jnp.int32, (A_HEADS, A_WIDTH), 1) // A_HEAD_DIM
    head_mask = jnp.where(hrow == hcol, 1.0, 0.0)
    qf = q_ref[0].astype(F32)
    qbd = jnp.concatenate([qf[t:t + 1, :] * head_mask for t in range(n_new)], axis=0).astype(BF16)
    nt = (((1,), (1,)), ((), ()))

    def softmax_step(state, logits, lanes, weighted_values):
        m_old, l_old, acc = state
        sel = sel_ref[0, :, lanes]
        sel_rows = jnp.concatenate([jnp.broadcast_to(sel[t:t + 1, :], (A_HEADS, sel.shape[1])) for t in range(n_new)],
                                   axis=0)
        x = jnp.where(sel_rows > 0.5, logits + bias_ref[:, lanes], NEG_BIG)
        m_new = jnp.maximum(m_old, jnp.max(x, axis=1, keepdims=True))
        alpha = jnp.exp(m_old - m_new)
        pm = jnp.exp(x - m_new)
        l_new = alpha * l_old + jnp.sum(pm, axis=1, keepdims=True)
        return m_new, l_new, alpha * acc + weighted_values(pm.astype(BF16))

    def chunk_body(c, state):
        sl = c % 2

        @pl.when(c + 1 < n_chunks)
        def _():
            for cp in chunk_copies(b, c + 1, 1 - sl):
                cp.start()

        @pl.when((c + 1 == n_chunks) & (b + 1 < pl.num_programs(0)))
        def _():
            for cp in chunk_copies(b + 1, 0, 1 - sl):
                cp.start()

        for cp in chunk_copies(b, c, sl):
            cp.wait()
        logits = jnp.concatenate([jnp.dot(qbd, kbuf[sl, p].astype(BF16), preferred_element_type=F32)
                                  for p in range(PAGES_PER_CHUNK)], axis=1)

        def weighted_values(pm):
            acc = jnp.zeros((rows, A_WIDTH), F32)
            for p in range(PAGES_PER_CHUNK):
                acc = acc + lax.dot_general(pm[:, p * PAGE_SIZE:(p + 1) * PAGE_SIZE], vbuf[sl, p].astype(BF16), nt,
                                            preferred_element_type=F32)
            return acc

        lanes = pl.ds(pl.multiple_of(c * keys_per_chunk, keys_per_chunk), keys_per_chunk)
        return softmax_step(state, logits, lanes, weighted_values)

    state = (jnp.full((rows, 1), NEG_BIG, F32), jnp.zeros((rows, 1), F32), jnp.zeros((rows, A_WIDTH), F32))
    state = lax.fori_loop(0, n_chunks, chunk_body, state)
    logits = lax.dot_general(qbd, kn_ref[0], nt, preferred_element_type=F32)
    _, l_fin, acc = softmax_step(state, logits, slice(n_pages * PAGE_SIZE, n_pages * PAGE_SIZE + NEW_PAD),
                                 lambda pm: jnp.dot(pm, vn_ref[0], preferred_element_type=F32))
    full_mask = jnp.concatenate([head_mask] * n_new, axis=0)
    o = (acc / l_fin) * full_mask
    o_ref[0] = jnp.sum(o.reshape(n_new, A_HEADS, A_WIDTH), axis=1).astype(o_ref.dtype)


def _sample_attend(page_table, q_bf, sel, bias_rows, k_new_pad, v_new_pad, cache_k_t, cache_v_t):
    nb, n_pages = page_table.shape
    n_new = q_bf.shape[1]
    assert n_pages % (2 * PAGES_PER_CHUNK) == 0
    assert cache_k_t.shape[1:] == (A_WIDTH, PAGE_SIZE)
    lk = n_pages * PAGE_SIZE + NEW_PAD
    page_buf = pltpu.VMEM((2, PAGES_PER_CHUNK, A_WIDTH, PAGE_SIZE), F32)
    grid_spec = pltpu.PrefetchScalarGridSpec(
        num_scalar_prefetch=1,
        grid=(nb,),
        in_specs=[pl.BlockSpec((1, n_new, A_WIDTH), lambda b, pt: (b, 0, 0)),
                  pl.BlockSpec((1, n_new, lk), lambda b, pt: (b, 0, 0)),
                  pl.BlockSpec((n_new * A_HEADS, lk), lambda b, pt: (0, 0), pipeline_mode=pl.Buffered(1)),
                  pl.BlockSpec((1, NEW_PAD, A_WIDTH), lambda b, pt: (b, 0, 0)),
                  pl.BlockSpec((1, NEW_PAD, A_WIDTH), lambda b, pt: (b, 0, 0)),
                  pl.BlockSpec(memory_space=pl.ANY),
                  pl.BlockSpec(memory_space=pl.ANY)],
        out_specs=pl.BlockSpec((1, n_new, A_WIDTH), lambda b, pt: (b, 0, 0)),
        scratch_shapes=[page_buf, page_buf, pltpu.SemaphoreType.DMA((2,)), pltpu.SemaphoreType.DMA((2,))],
    )
    return pl.pallas_call(
        functools.partial(_sample_attend_kernel, n_pages=n_pages, n_new=n_new),
        grid_spec=grid_spec,
        out_shape=jax.ShapeDtypeStruct((nb, n_new, A_WIDTH), BF16),
        compiler_params=_cparams(("arbitrary",)),
        name="sample_attend",
    )(page_table, q_bf, sel, bias_rows, k_new_pad, v_new_pad, cache_k_t, cache_v_t)


def _dsa_sample(qa_bf, kab, vab, qi_bf, kiw, rel_bias, cache_k, cache_v, cache_ki, page_table, n_new):
    nb, n_pages = page_table.shape
    n_past = n_pages * PAGE_SIZE
    assert n_new <= NEW_PAD and n_past >= REL_MAX_DIST
    topk = min(INDEX_TOPK, (n_past + n_new) // 4)
    qi_rows = qi_bf.reshape(nb, n_new, IDX_HEADS, IDX_DIM).transpose(0, 2, 1, 3).reshape(nb, IDX_HEADS * n_new, IDX_DIM)
    wi = kiw[:, IDX_DIM:IDX_DIM + IDX_HEADS] * INDEX_SCALE
    w_rows = jnp.broadcast_to(wi.reshape(nb, n_new, IDX_HEADS).transpose(0, 2, 1).reshape(nb, IDX_HEADS * n_new, 1),
                              (nb, IDX_HEADS * n_new, LANES))
    pad_rows = lambda a: jnp.pad(a, ((0, 0), (0, NEW_PAD - a.shape[1]), (0, 0)))
    ki_new = pad_rows(kiw[:, :IDX_DIM].astype(BF16).reshape(nb, n_new, IDX_DIM))
    n_pool = cache_k.shape[0]
    cache_ki_t = cache_ki.transpose(0, 2, 1)
    cache_k_t = cache_k.transpose(0, 2, 3, 1).reshape(n_pool, A_WIDTH, PAGE_SIZE)
    cache_v_t = cache_v.transpose(0, 2, 3, 1).reshape(n_pool, A_WIDTH, PAGE_SIZE)
    scores = _sample_scores(page_table, qi_rows, w_rows, ki_new, cache_ki_t)
    lk = scores.shape[-1]
    sel = _sample_select(scores.reshape(nb * n_new, lk), topk, n_past, n_new, rows=32)
    n_tail = REL_MAX_DIST + NEW_PAD
    table = _bias_by_distance(rel_bias, REL_MAX_DIST + n_tail)
    dist = (n_past + jnp.arange(n_new))[:, None] - (n_past - REL_MAX_DIST + jnp.arange(n_tail))[None, :]
    tail = table[:, jnp.clip(dist, 0, table.shape[1] - 1)].transpose(1, 0, 2)
    far = jnp.broadcast_to(table[None, :, -1:], (n_new, A_HEADS, n_past - REL_MAX_DIST))
    bias_rows = jnp.concatenate([far, tail], axis=2).reshape(n_new * A_HEADS, lk)
    o = _sample_attend(page_table, qa_bf.reshape(nb, n_new, A_WIDTH), sel.reshape(nb, n_new, lk), bias_rows,
                       pad_rows(kab.reshape(nb, n_new, A_WIDTH)), pad_rows(vab.reshape(nb, n_new, A_WIDTH)),
                       cache_k_t, cache_v_t)
    return o.reshape(nb * n_new, A_WIDTH)


def kernel(x_prompt, x_sample, cache_k, cache_v, cache_idx_k, state_gla, page_table, norm_mix, w_in, w_gate_up, b_gate, gla_norm, w_branch_a, w_branch_b, w_out, norm_ffn, w_router, b_router, w_exp_up, b_exp_up, w_exp_down, b_exp_down, rel_bias, norm_final):
    batch, seq, d = x_prompt.shape
    nb, n_new, _ = x_sample.shape
    assert w_in.shape[0] == 1, "the final norm is fused into the single layer's MoE kernel"
    layer = 0
    w_packed = _pack_w_in(w_in[layer])

    def trunk(x2d, attend, gla_args, tm_moe):
        tokens = x2d.shape[0]
        tm = min(512, tokens)
        qa, ka, kab, va, vab, qi, kiw, qb, kb, vb, r, glow, ga, gb = _inproj(x2d, norm_mix[layer], w_packed, tm)
        oa = attend(qa, kab, vab, qi, kiw)
        ob, state = gla_args(qb, kb, vb, r, glow)
        x2, h2, comb = _merge(oa, ob, ga, gb, x2d, w_branch_a[layer], w_branch_b[layer], w_out[layer], norm_ffn[layer],
                              w_router[layer], b_router[layer], tm)
        y = _moe(h2, comb, x2, w_exp_up[layer], b_exp_up[layer], w_exp_down[layer], b_exp_down[layer], norm_final,
                 min(tm_moe, tokens))
        return y, ka, va, kiw[:, :IDX_DIM], state

    gla_w = (w_gate_up[layer], b_gate[layer], gla_norm[layer])

    def gla_prompt(qb, kb, vb, r, glow):
        s0 = jnp.zeros((batch, B_HEADS, B_KEY_DIM, B_VAL_DIM), F32)
        return _gla(qb, kb, vb, r, glow, *gla_w, s0, batch, seq, GLA_CHUNK)

    def gla_sample(qb, kb, vb, r, glow):
        pad = lambda a: jnp.pad(a.reshape(nb, n_new, -1), ((0, 0), (0, GLA_CHUNK - n_new), (0, 0))).reshape(nb * GLA_CHUNK, -1)
        o, state = _gla(pad(qb), pad(kb), pad(vb), pad(r), pad(glow), *gla_w, state_gla[layer].astype(F32), nb, GLA_CHUNK,
                        n_new)
        return o.reshape(nb, GLA_CHUNK, -1)[:, :n_new].reshape(nb * n_new, -1), state

    yp, kp, vp, kip, sp = trunk(
        x_prompt.reshape(batch * seq, d),
        lambda qa, kab, vab, qi, kiw: _dsa_prompt(qa, kab, vab, qi, kiw, rel_bias, batch, seq),
        gla_prompt, 1024)
    ys, ks, vs, kis, ss = trunk(
        x_sample.reshape(nb * n_new, d),
        lambda qa, kab, vab, qi, kiw: _dsa_sample(qa, kab, vab, qi, kiw, rel_bias, cache_k[layer], cache_v[layer],
                                                  cache_idx_k[layer], page_table, n_new),
        gla_sample, 512)
    heads = (A_HEADS, A_HEAD_DIM)
    return (yp.reshape(batch, seq, d), ys.reshape(nb, n_new, d),
            kp.reshape(1, batch, seq, *heads), vp.reshape(1, batch, seq, *heads), kip.reshape(1, batch, seq, IDX_DIM),
            sp.astype(x_prompt.dtype)[None],
            ks.reshape(1, nb, n_new, *heads), vs.reshape(1, nb, n_new, *heads), kis.reshape(1, nb, n_new, IDX_DIM),
            ss.astype(state_gla.dtype)[None])
```

```python
import functools
import math

import numpy as np
import jax
import jax.numpy as jnp
from jax import lax
from jax.experimental import pallas as pl
from jax.experimental.pallas import tpu as pltpu

F32 = jnp.float32
BF16 = jnp.bfloat16

A_HEADS = 8
A_HEAD_DIM = 64
A_WIDTH = A_HEADS * A_HEAD_DIM
IDX_HEADS = 4
IDX_DIM = 64
INDEX_TOPK = 256
INDEX_SCALE = (IDX_HEADS * IDX_DIM) ** -0.5
REL_BUCKETS = 32
REL_MAX_DIST = 128
B_HEADS = 4
B_KEY_DIM = 64
B_VAL_DIM = 128
B_QK_WIDTH = B_HEADS * B_KEY_DIM
B_V_WIDTH = B_HEADS * B_VAL_DIM
GATE_RANK = 16
GATE_TAU = 16.0
GLA_CHUNK = 64
GLA_SUB = 16
TOP_K = 4
SWIGLU_ALPHA = 1.702
SWIGLU_LIMIT = 7.0
EPS = 1e-6
PAGE_SIZE = 128

LANES = 128
NEG_BIG = -1e30
INT_MIN = -2 ** 31
VMEM_LIMIT = 56 * 1024 * 1024


def _cparams(sem):
    return pltpu.CompilerParams(dimension_semantics=sem, vmem_limit_bytes=VMEM_LIMIT)


def _rms(xf, g):
    return xf * lax.rsqrt(jnp.mean(xf * xf, axis=-1, keepdims=True) + EPS) * g


_C_QA, _C_KA, _C_VA = 0, 512, 1024
_C_QI, _C_KIW = 1536, 1792
_C_QB, _C_KB, _C_VB, _C_R, _C_GLOW = 1920, 2176, 2432, 2944, 3456
_C_GA, _C_GB = 3584, 4608
_C_END = 5632


def _pack_w_in(w_in):
    d = w_in.shape[0]
    o = np.cumsum([0, 512, 512, 512, 256, 64, 4, 256, 256, 512, 16, 512, 1024, 1024]).tolist()
    seg = lambda i: w_in[:, o[i]:o[i + 1]]
    z = lambda n: jnp.zeros((d, n), w_in.dtype)
    packed = jnp.concatenate(
        [seg(0), seg(1), seg(2), seg(3), seg(4), seg(5), z(60),
         seg(6), seg(7), seg(8), seg(10), seg(9), z(112), seg(11), seg(12)], axis=1)
    assert packed.shape[1] == _C_END
    return packed.astype(BF16)


def _inproj_kernel(x_ref, g_ref, w_ref, qa_ref, ka_ref, kab_ref, va_ref, vab_ref, qi_ref, kiw_ref,
                   qb_ref, kb_ref, vb_ref, r_ref, glow_ref, ga_ref, gb_ref):
    h = _rms(x_ref[...], g_ref[...]).astype(BF16)

    def proj(a, b):
        return jnp.dot(h, w_ref[:, a:b], preferred_element_type=F32)

    za = proj(_C_QA, _C_QI)
    qa_ref[...] = (za[:, 0:512] * (A_HEAD_DIM ** -0.5)).astype(BF16)
    ka = za[:, 512:1024]
    va = za[:, 1024:1536]
    ka_ref[...] = ka
    kab_ref[...] = ka.astype(BF16)
    va_ref[...] = va
    vab_ref[...] = va.astype(BF16)
    zi = proj(_C_QI, _C_QB)
    qi_ref[...] = zi[:, 0:256].astype(BF16)
    kiw_ref[...] = zi[:, 256:384]
    zb = proj(_C_QB, _C_GA)
    qb_ref[...] = zb[:, 0:256]
    kb_ref[...] = zb[:, 256:512]
    vb_ref[...] = zb[:, 512:1024]
    r_ref[...] = zb[:, 1024:1536]
    glow_ref[...] = zb[:, 1536:1664]
    zg = proj(_C_GA, _C_END)
    ga_ref[...] = zg[:, 0:1024]
    gb_ref[...] = zg[:, 1024:2048]


def _inproj(x2d, g, w_packed, tm):
    t, d = x2d.shape
    widths = [(512, BF16), (512, F32), (512, BF16), (512, F32), (512, BF16), (256, BF16), (128, F32),
              (256, F32), (256, F32), (512, F32), (512, F32), (128, F32), (1024, F32), (1024, F32)]
    return pl.pallas_call(
        _inproj_kernel,
        grid=(t // tm,),
        in_specs=[pl.BlockSpec((tm, d), lambda i: (i, 0)),
                  pl.BlockSpec((1, d), lambda i: (0, 0)),
                  pl.BlockSpec((d, _C_END), lambda i: (0, 0))],
        out_specs=[pl.BlockSpec((tm, w), lambda i: (i, 0)) for w, _ in widths],
        out_shape=[jax.ShapeDtypeStruct((t, w), dt) for w, dt in widths],
        compiler_params=_cparams(("arbitrary",)),
        name="inproj",
    )(x2d, g.reshape(1, d), w_packed)


KEY_NEG_INF = -2139095041


def _key_to_float(key):
    key = jnp.maximum(key, KEY_NEG_INF)
    return pltpu.bitcast(key ^ ((key >> 31) & 0x7FFFFFFF), F32)


def _canonical_zero(s):
    return jnp.where(s == 0.0, 0.0, s)


def _t5_bucket_np(dist):
    max_exact = REL_BUCKETS // 2
    ratio = np.log(np.maximum(dist, 1).astype(np.float32) / np.float32(max_exact)) / np.float32(
        math.log(REL_MAX_DIST / max_exact))
    large = np.minimum(max_exact + (ratio * (REL_BUCKETS - max_exact)).astype(np.int32), REL_BUCKETS - 1)
    return np.where(dist < max_exact, dist, large).astype(np.int32)


def _bias_by_distance(rel_bias, n):
    dist = jnp.arange(n, dtype=jnp.int32)
    max_exact = REL_BUCKETS // 2
    log_ratio = jnp.log(jnp.maximum(dist, 1).astype(F32) / max_exact) / math.log(REL_MAX_DIST / max_exact)
    large = jnp.minimum(max_exact + (log_ratio * (REL_BUCKETS - max_exact)).astype(jnp.int32), REL_BUCKETS - 1)
    bucket = jnp.where(dist < max_exact, dist, large)
    return rel_bias[bucket].T.astype(F32)


def _dsa_prompt_kernel(rel_ref, qT_ref, qiT_ref, wT_ref, k_ref, vT_ref, ki_ref, bkt_ref, o_ref,
                       s_scr, qbd_scr, acc_scr, m_scr, l_scr, bias_scr, mask_scr, *, tq, tk, topk):
    i = pl.program_id(1)

    @pl.when((pl.program_id(0) == 0) & (i == 0))
    def _():
        for which in range(2 * (tk // tq)):
            bkt = bkt_ref[which]

            def one_head(h, carry):
                far = rel_ref[REL_BUCKETS - 1, h]
                bias_scr[which, h] = lax.fori_loop(
                    0, REL_BUCKETS - 1, lambda b, t: jnp.where(bkt == b, rel_ref[b, h] - far, t),
                    jnp.zeros((tk, tq), F32))
                return carry

            lax.fori_loop(0, A_HEADS, one_head, 0)

    diag = (i * tq) // tk
    nkb = diag + 1
    qpos = i * tq + lax.broadcasted_iota(jnp.int32, (1, tq), 1)
    krow = lax.broadcasted_iota(jnp.int32, (tk, 1), 0)

    qiT = qiT_ref[0]
    qi_st = jnp.concatenate([qiT[h * IDX_DIM:(h + 1) * IDX_DIM, :] for h in range(IDX_HEADS)], axis=1)
    w = wT_ref[0]

    def score_block(j, carry):
        s = jnp.dot(ki_ref[0, j], qi_st, preferred_element_type=F32)
        acc = jnp.maximum(s[:, 0:tq], 0.0) * w[0:1, :]
        for h in range(1, IDX_HEADS):
            acc = acc + jnp.maximum(s[:, h * tq:(h + 1) * tq], 0.0) * w[h:h + 1, :]
        acc = jnp.where(j * tk + krow <= qpos, acc, -jnp.inf)
        s_scr[pl.ds(pl.multiple_of(j * tk, tk), tk), :] = _canonical_zero(acc)
        return carry

    lax.fori_loop(0, nkb, score_block, 0)

    def count(cand, strictly):
        def body(j, c):
            blk = s_scr[pl.ds(pl.multiple_of(j * tk, tk), tk), :]
            hit = jnp.where(blk > cand if strictly else blk >= cand, 1, 0).astype(jnp.int32)
            return c + jnp.sum(hit.reshape(tk // 8, 8, tq), axis=0)
        c = lax.fori_loop(0, nkb, body, jnp.zeros((8, tq), jnp.int32))
        return jnp.sum(c, axis=0, keepdims=True)

    def bisect(step, key):
        cand = key + lax.shift_left(jnp.int32(1), 31 - step)
        return jnp.where(count(_key_to_float(cand), False) >= topk, cand, key)

    thr = _key_to_float(lax.fori_loop(0, 32, bisect, jnp.full((1, tq), INT_MIN, jnp.int32)))
    n_take = (topk - count(thr, True)).astype(F32)

    rowid = lax.broadcasted_iota(jnp.int32, (2 * A_HEAD_DIM, tq), 0)
    for p in range(A_HEADS // 2):
        slab = qT_ref[0, p * 128:(p + 1) * 128, :].astype(F32)
        top = jnp.where(rowid < A_HEAD_DIM, slab, 0.0)
        qbd_scr[p] = jnp.concatenate([top, slab - top], axis=1).astype(BF16)

    acc_scr[...] = jnp.zeros_like(acc_scr)
    m_scr[...] = jnp.full_like(m_scr, NEG_BIG)
    l_scr[...] = jnp.zeros_like(l_scr)
    ri = lax.broadcasted_iota(jnp.int32, (tk, tk), 0)
    ci = lax.broadcasted_iota(jnp.int32, (tk, tk), 1)
    ltri = jnp.where(ci < ri, 1.0, 0.0).astype(BF16)

    def attend_block(j, tie_seen, near):
        keyblk = s_scr[pl.ds(pl.multiple_of(j * tk, tk), tk), :]
        eq = jnp.where(keyblk == thr, 1.0, 0.0)
        before = tie_seen + jnp.dot(ltri, eq.astype(BF16), preferred_element_type=F32)
        take = jnp.where(keyblk > thr, 1.0, jnp.where(before < n_take, eq, 0.0))
        if near:
            take = jnp.where(j * tk + krow <= qpos, take, 0.0)
            which = 2 * ((i * tq) % tk // tq) + (diag - j)
        mask_scr[...] = jnp.where(take > 0.5, 0.0, NEG_BIG)
        for p in range(A_HEADS // 2):
            lg = jnp.dot(k_ref[0, j, :, p * 128:(p + 1) * 128], qbd_scr[p], preferred_element_type=F32)
            for u in range(2):
                h = 2 * p + u
                x = lg[:, u * tq:(u + 1) * tq] + mask_scr[...]
                if near:
                    x = x + bias_scr[which, h]
                m_old = m_scr[h:h + 1, :]
                m_new = jnp.maximum(m_old, jnp.max(x, axis=0, keepdims=True))
                alpha = jnp.exp(m_old - m_new)
                pm = jnp.exp(x - m_new)
                l_scr[h:h + 1, :] = alpha * l_scr[h:h + 1, :] + jnp.sum(pm, axis=0, keepdims=True)
                pv = jnp.dot(vT_ref[0, j, h * A_HEAD_DIM:(h + 1) * A_HEAD_DIM, :], pm.astype(BF16),
                             preferred_element_type=F32)
                rows = slice(h * A_HEAD_DIM, (h + 1) * A_HEAD_DIM)
                acc_scr[rows, :] = alpha * acc_scr[rows, :] + pv
                m_scr[h:h + 1, :] = m_new
        return tie_seen + jnp.sum(eq, axis=0, keepdims=True)

    n_far = jnp.maximum(diag - 1, 0)
    tie_seen = lax.fori_loop(0, n_far, functools.partial(attend_block, near=False), jnp.zeros((1, tq), F32))
    lax.fori_loop(n_far, nkb, functools.partial(attend_block, near=True), tie_seen)

    for h in range(A_HEADS):
        rows = slice(h * A_HEAD_DIM, (h + 1) * A_HEAD_DIM)
        acc_scr[rows, :] = acc_scr[rows, :] / l_scr[h:h + 1, :]
    o_ref[0] = acc_scr[...].T.astype(o_ref.dtype)


def _dsa_prompt(qa_bf, ka_bf, va_bf, qi_bf, kiw, rel_bias, batch, seq):
    tq, tk = 256, 256
    assert seq % tk == 0 and tk % tq == 0
    topk = min(INDEX_TOPK, seq // 4)
    nk = seq // tk
    qT = qa_bf.reshape(batch, seq, A_WIDTH).transpose(0, 2, 1)
    qiT = qi_bf.reshape(batch, seq, IDX_HEADS * IDX_DIM).transpose(0, 2, 1)
    wi = kiw[:, IDX_DIM:IDX_DIM + IDX_HEADS] * INDEX_SCALE
    wT = jnp.pad(wi.reshape(batch, seq, IDX_HEADS).transpose(0, 2, 1), ((0, 0), (0, 8 - IDX_HEADS), (0, 0)))
    k4 = ka_bf.reshape(batch, nk, tk, A_WIDTH)
    vT4 = va_bf.reshape(batch, nk, tk, A_WIDTH).transpose(0, 1, 3, 2)
    ki4 = kiw[:, :IDX_DIM].astype(BF16).reshape(batch, nk, tk, IDX_DIM)
    assert REL_MAX_DIST <= tk
    d0 = np.arange(tq)[None, :] - np.arange(tk)[:, None]
    bkt = jnp.asarray(np.stack([_t5_bucket_np(np.maximum(d0 + off * tq + kind * tk, 0))
                                for off in range(tk // tq) for kind in range(2)]), jnp.int32)

    once = dict(pipeline_mode=pl.Buffered(1))
    kern = functools.partial(_dsa_prompt_kernel, tq=tq, tk=tk, topk=topk)
    out = pl.pallas_call(
        kern,
        grid=(batch, seq // tq),
        in_specs=[pl.BlockSpec(memory_space=pltpu.SMEM),
                  pl.BlockSpec((1, A_WIDTH, tq), lambda b, i: (b, 0, i)),
                  pl.BlockSpec((1, IDX_HEADS * IDX_DIM, tq), lambda b, i: (b, 0, i)),
                  pl.BlockSpec((1, 8, tq), lambda b, i: (b, 0, i)),
                  pl.BlockSpec((1, nk, tk, A_WIDTH), lambda b, i: (b, 0, 0, 0), **once),
                  pl.BlockSpec((1, nk, A_WIDTH, tk), lambda b, i: (b, 0, 0, 0), **once),
                  pl.BlockSpec((1, nk, tk, IDX_DIM), lambda b, i: (b, 0, 0, 0), **once),
                  pl.BlockSpec(bkt.shape, lambda b, i: (0, 0, 0), **once)],
        out_specs=pl.BlockSpec((1, tq, A_WIDTH), lambda b, i: (b, i, 0)),
        out_shape=jax.ShapeDtypeStruct((batch, seq, A_WIDTH), BF16),
        scratch_shapes=[pltpu.VMEM((seq, tq), F32),
                        pltpu.VMEM((A_HEADS // 2, 2 * A_HEAD_DIM, 2 * tq), BF16),
                        pltpu.VMEM((A_WIDTH, tq), F32),
                        pltpu.VMEM((A_HEADS, tq), F32),
                        pltpu.VMEM((A_HEADS, tq), F32),
                        pltpu.VMEM((bkt.shape[0], A_HEADS, tk, tq), F32),
                        pltpu.VMEM((tk, tq), F32)],
        compiler_params=_cparams(("arbitrary", "arbitrary")),
        name="dsa_prompt",
    )(rel_bias.astype(F32), qT, qiT, wT, k4, vT4, ki4, bkt)
    return out.reshape(batch * seq, A_WIDTH)


def _gla_kernel(q_ref, k_ref, v_ref, r_ref, glow_ref, wg_ref, bg_ref, gn_ref, s0_ref, o_ref, s_out_ref,
                sT_ref, q_scr, k_scr, b_scr, att_scr, *, n_valid):
    C = GLA_CHUNK
    hp = lax.Precision.HIGHEST

    @pl.when(pl.program_id(1) == 0)
    def _():
        for p in range(B_HEADS // 2):
            sT_ref[p] = s0_ref[0, 2 * p:2 * p + 2].reshape(LANES, B_VAL_DIM).T

    g = jnp.dot(glow_ref[0], wg_ref[...], precision=hp, preferred_element_type=F32) + bg_ref[...]
    log_a = (jnp.minimum(g, 0.0) - jnp.log1p(jnp.exp(-jnp.abs(g)))) * (1.0 / GATE_TAU)
    if n_valid < C:
        log_a = jnp.where(lax.broadcasted_iota(jnp.int32, (C, 1), 0) < n_valid, log_a, 0.0)
        att_scr[...] = jnp.zeros_like(att_scr)
    tri = jnp.where(lax.broadcasted_iota(jnp.int32, (C, C), 1) <= lax.broadcasted_iota(jnp.int32, (C, C), 0), 1.0, 0.0)
    b = jnp.dot(tri, log_a, precision=hp, preferred_element_type=F32)
    q = q_ref[0] * (B_KEY_DIM ** -0.5)
    k = k_ref[0]
    v = v_ref[0]
    q_scr[...] = q
    k_scr[...] = k
    b_scr[...] = b
    eb = jnp.exp(b)
    qe = q * eb
    kdec = k * jnp.exp(b[C - 1:C, :] - b)
    eb_last = eb[C - 1:C, :]

    lane1 = lax.broadcasted_iota(jnp.int32, (1, LANES), 1)
    head_mask = (jnp.where(lane1 < B_KEY_DIM, 1.0, 0.0), jnp.where(lane1 >= B_KEY_DIM, 1.0, 0.0))
    lane_c = lax.broadcasted_iota(jnp.int32, (C, LANES), 1) % B_KEY_DIM
    row_c = lax.broadcasted_iota(jnp.int32, (C, LANES), 0)
    causal = jnp.where(lane_c <= row_c, 1.0, 0.0)
    lane_s = lax.broadcasted_iota(jnp.int32, (GLA_SUB, LANES), 1) % B_KEY_DIM
    row_s = lax.broadcasted_iota(jnp.int32, (GLA_SUB, LANES), 0)
    same_head = jnp.where(lax.broadcasted_iota(jnp.int32, (LANES, LANES), 0) // B_KEY_DIM
                          == lax.broadcasted_iota(jnp.int32, (LANES, LANES), 1) // B_KEY_DIM, 1.0, 0.0).astype(BF16)
    tn = (((0,), (0,)), ((), ()))
    nt = (((1,), (1,)), ((), ()))

    for p in range(B_HEADS // 2):
        sl = slice(p * LANES, (p + 1) * LANES)
        kp = k[:, sl]
        bp = b[:, sl]
        qp = q[:, sl]

        off_rows = [jnp.zeros((GLA_SUB, LANES), F32)]
        for big_i in range(1, C // GLA_SUB):
            lo_row = big_i * GLA_SUB
            if lo_row >= n_valid:
                off_rows.append(jnp.zeros((GLA_SUB, LANES), F32))
                continue
            rho = b_scr[lo_row - 1:lo_row, sl]
            qt = qp[lo_row:lo_row + GLA_SUB] * jnp.exp(bp[lo_row:lo_row + GLA_SUB] - rho)
            kt = kp * jnp.exp(jnp.minimum(rho - bp, 0.0))
            kt2 = jnp.concatenate([kt * head_mask[0], kt * head_mask[1]], axis=0)
            qh = qt.astype(BF16)
            ql = (qt - qh.astype(F32)).astype(BF16)
            kh = kt2.astype(BF16)
            kl = (kt2 - kh.astype(F32)).astype(BF16)
            a = (lax.dot_general(qh, kh, nt, preferred_element_type=F32)
                 + lax.dot_general(qh, kl, nt, preferred_element_type=F32)
                 + lax.dot_general(ql, kh, nt, preferred_element_type=F32))
            off_rows.append(jnp.where(lane_s < lo_row, a, 0.0))
        att_off = jnp.concatenate(off_rows, axis=0)

        def intra_rows(grp, carry):
            base = pl.multiple_of(grp * 8, 8)
            sub0 = pl.multiple_of(grp // (GLA_SUB // 8) * GLA_SUB, GLA_SUB)
            b8 = b_scr[pl.ds(base, 8), sl]
            q8 = q_scr[pl.ds(base, 8), sl]
            ksub = k_scr[pl.ds(sub0, GLA_SUB), sl]
            bsub = b_scr[pl.ds(sub0, GLA_SUB), sl]
            on_diag = jnp.where(lane_s == sub0 + row_s, 1.0, 0.0)
            xs = []
            for rr in range(8):
                if rr >= n_valid:
                    xs.append(jnp.zeros((GLA_SUB, LANES), F32))
                else:
                    xs.append((q8[rr:rr + 1, :] * ksub) * jnp.exp(jnp.minimum(b8[rr:rr + 1, :] - bsub, 0.0)))
            x = jnp.concatenate(xs, axis=0)
            xh = x.astype(BF16)
            xl = (x - xh.astype(F32)).astype(BF16)
            seg = (jnp.dot(xh, same_head, preferred_element_type=F32)
                   + jnp.dot(xl, same_head, preferred_element_type=F32))
            att_scr[pl.ds(base, 8), sl] = jnp.concatenate(
                [jnp.sum(seg[rr * GLA_SUB:(rr + 1) * GLA_SUB] * on_diag, axis=0, keepdims=True) for rr in range(8)], axis=0)
            return carry

        lax.fori_loop(0, (n_valid + 7) // 8, intra_rows, 0)
        att = (att_scr[:, sl] + att_off) * causal
        sT = sT_ref[p]
        sT_bf = sT.astype(BF16)
        v_pair = jnp.concatenate([v[:, (2 * p) * B_VAL_DIM:(2 * p + 1) * B_VAL_DIM],
                                  v[:, (2 * p + 1) * B_VAL_DIM:(2 * p + 2) * B_VAL_DIM]], axis=0).astype(BF16)
        upd = jnp.zeros((B_VAL_DIM, LANES), F32)
        for u in range(2):
            h = 2 * p + u
            hs = slice(h * B_VAL_DIM, (h + 1) * B_VAL_DIM)
            o = lax.dot_general((qe[:, sl] * head_mask[u]).astype(BF16), sT_bf, nt, preferred_element_type=F32)
            o = o + jnp.dot((att * head_mask[u]).astype(BF16), v_pair, preferred_element_type=F32)
            o = o * lax.rsqrt(jnp.mean(o * o, axis=-1, keepdims=True) + EPS)
            rh = r_ref[0, :, hs]
            o_ref[0, :, hs] = (o * gn_ref[:, hs] * (rh * jax.nn.sigmoid(rh))).astype(o_ref.dtype)
            upd = upd + lax.dot_general(v[:, hs].astype(BF16), (kdec[:, sl] * head_mask[u]).astype(BF16), tn,
                                        preferred_element_type=F32)
        s_new = sT * eb_last[:, sl] + upd
        sT_ref[p] = s_new

        @pl.when(pl.program_id(1) == pl.num_programs(1) - 1)
        def _():
            s_out_ref[0, 2 * p:2 * p + 2] = s_new.T.reshape(2, B_KEY_DIM, B_VAL_DIM)


def _gla(qb, kb, vb, r, glow, w_gate_up, b_gate, gla_norm, s0, nb, length, n_valid):
    C = GLA_CHUNK
    assert length % C == 0 and B_KEY_DIM * 2 == LANES and C == B_KEY_DIM and B_VAL_DIM == LANES
    rs = lambda a: a.reshape(nb, length, a.shape[-1])
    wg = jnp.pad(w_gate_up, ((0, LANES - GATE_RANK), (0, 0)))
    tok = lambda w: pl.BlockSpec((1, C, w), lambda b, c: (b, c, 0))
    full = lambda shape: pl.BlockSpec(shape, lambda b, c: (0,) * len(shape))
    st = pl.BlockSpec((1, B_HEADS, B_KEY_DIM, B_VAL_DIM), lambda b, c: (b, 0, 0, 0))
    o, state = pl.pallas_call(
        functools.partial(_gla_kernel, n_valid=n_valid),
        grid=(nb, length // C),
        in_specs=[tok(B_QK_WIDTH), tok(B_QK_WIDTH), tok(B_V_WIDTH), tok(B_V_WIDTH), tok(LANES),
                  full((LANES, B_QK_WIDTH)), full((1, B_QK_WIDTH)), full((1, B_V_WIDTH)), st],
        out_specs=[tok(B_V_WIDTH), st],
        out_shape=[jax.ShapeDtypeStruct((nb, length, B_V_WIDTH), BF16),
                   jax.ShapeDtypeStruct((nb, B_HEADS, B_KEY_DIM, B_VAL_DIM), F32)],
        scratch_shapes=[pltpu.VMEM((B_HEADS // 2, B_VAL_DIM, LANES), F32)] + [pltpu.VMEM((C, B_QK_WIDTH), F32)] * 4,
        compiler_params=_cparams(("arbitrary", "arbitrary")),
        name="gla",
    )(rs(qb), rs(kb), rs(vb), rs(r), rs(glow), wg, b_gate.reshape(1, -1), gla_norm.reshape(1, -1), s0)
    return o.reshape(nb * length, B_V_WIDTH), state


def _merge_kernel(oa_ref, ob_ref, ga_ref, gb_ref, x_ref, wa_ref, wb_ref, wo_ref, gf_ref, wr_ref, br_ref,
                  x2_ref, h2_ref, comb_ref, *, n_experts):
    ya = jnp.dot(oa_ref[...], wa_ref[...], preferred_element_type=F32)
    yb = jnp.dot(ob_ref[...], wb_ref[...], preferred_element_type=F32)
    merged = jax.nn.sigmoid(ga_ref[...]) * ya + jax.nn.sigmoid(gb_ref[...]) * yb
    x2 = x_ref[...] + jnp.dot(merged.astype(BF16), wo_ref[...], preferred_element_type=F32)
    x2_ref[...] = x2
    h2 = _rms(x2, gf_ref[...])
    h2_ref[...] = h2.T.astype(BF16)
    logits = jnp.dot(h2, wr_ref[...], precision=lax.Precision.HIGHEST, preferred_element_type=F32) + br_ref[...]
    lane = lax.broadcasted_iota(jnp.int32, logits.shape, 1).astype(F32)
    work = jnp.where(lane < n_experts, logits, -jnp.inf)
    picks = []
    for _ in range(TOP_K):
        m = jnp.max(work, axis=-1, keepdims=True)
        first = jnp.min(jnp.where(work == m, lane, float(LANES)), axis=-1, keepdims=True)
        hit = lane == first
        picks.append((m, hit))
        work = jnp.where(hit, -jnp.inf, work)
    es = [jnp.exp(m - picks[0][0]) for m, _ in picks]
    inv = 1.0 / (es[0] + es[1] + es[2] + es[3])
    comb = jnp.zeros_like(logits)
    for e, (_, hit) in zip(es, picks):
        comb = jnp.where(hit, e * inv, comb)
    comb_ref[...] = comb.T


def _merge(oa, ob, ga, gb, x2d, w_branch_a, w_branch_b, w_out, norm_ffn, w_router, b_router, tm):
    t, d = x2d.shape
    n_experts = w_router.shape[1]
    assert n_experts <= LANES and TOP_K == 4
    wr = jnp.pad(w_router, ((0, 0), (0, LANES - n_experts)))
    br = jnp.pad(b_router, (0, LANES - n_experts)).reshape(1, LANES)
    tok = lambda w: pl.BlockSpec((tm, w), lambda i: (i, 0))
    full = lambda a: pl.BlockSpec(a.shape, lambda i: (0,) * a.ndim)
    wa, wb, wo, gf = w_branch_a.astype(BF16), w_branch_b.astype(BF16), w_out.astype(BF16), norm_ffn.reshape(1, d)
    return pl.pallas_call(
        functools.partial(_merge_kernel, n_experts=n_experts),
        grid=(t // tm,),
        in_specs=[tok(A_WIDTH), tok(B_V_WIDTH), tok(d), tok(d), tok(d),
                  full(wa), full(wb), full(wo), full(gf), full(wr), full(br)],
        out_specs=[tok(d), pl.BlockSpec((d, tm), lambda i: (0, i)), pl.BlockSpec((LANES, tm), lambda i: (0, i))],
        out_shape=[jax.ShapeDtypeStruct((t, d), F32), jax.ShapeDtypeStruct((d, t), BF16),
                   jax.ShapeDtypeStruct((LANES, t), F32)],
        compiler_params=_cparams(("arbitrary",)),
        name="merge",
    )(oa, ob, ga, gb, x2d, wa, wb, wo, gf, wr, br)


MOE_CAP = 256


def _moe_kernel(hT_ref, combT_ref, x2_ref, wuT_ref, bu_ref, wdT_ref, bd_ref, gfin_ref, o_ref,
                outT_scr, rank_scr, comb_scr, *, n_experts):
    e = pl.program_id(1)
    d, tm = hT_ref.shape
    de = wdT_ref.shape[2]

    @pl.when(e == 0)
    def _():
        outT_scr[...] = jnp.zeros_like(outT_scr)
        ra = lax.broadcasted_iota(jnp.int32, (LANES, LANES), 0)
        ca = lax.broadcasted_iota(jnp.int32, (LANES, LANES), 1)
        upper = jnp.where(ra < ca, 1.0, 0.0).astype(BF16)
        seen = jnp.zeros((LANES, 1), F32)
        for c in range(tm // LANES):
            sl = slice(c * LANES, (c + 1) * LANES)
            comb_c = combT_ref[:, sl]
            routed = jnp.where(comb_c > 0.0, 1.0, 0.0)
            rank = seen + jnp.dot(routed.astype(BF16), upper, preferred_element_type=F32)
            rank = jnp.where(routed > 0.5, rank, -1.0)
            for x in range(n_experts):
                rank_scr[x, :, sl] = jnp.broadcast_to(rank[x:x + 1, :], (8, LANES))
                comb_scr[x, :, sl] = jnp.broadcast_to(comb_c[x:x + 1, :], (8, LANES))
            seen = seen + jnp.sum(routed, axis=1, keepdims=True)

    rank_rows = rank_scr[e]
    rank_row = rank_rows[0:1, :]
    comb_row = comb_scr[e][0:1, :]
    reps = MOE_CAP // LANES
    count = (jnp.sum(jnp.where(rank_rows >= 0.0, 1.0, 0.0)) * 0.125).astype(jnp.int32)
    slot = lax.broadcasted_iota(jnp.int32, (MOE_CAP, tm), 0).astype(F32)
    nt = (((1,), (1,)), ((), ()))

    def one_pass(n, carry):
        onehot = jnp.where(rank_row - (n * MOE_CAP).astype(F32) == slot, 1.0, 0.0).astype(BF16)
        xs = lax.dot_general(hT_ref[...], onehot, nt, preferred_element_type=F32).astype(BF16)
        z = jnp.dot(wuT_ref[0], xs, preferred_element_type=F32) + jnp.tile(bu_ref[0], (1, reps))
        glu = jnp.minimum(z[:de], SWIGLU_LIMIT)
        lin = jnp.clip(z[de:], -SWIGLU_LIMIT, SWIGLU_LIMIT)
        act = glu * jax.nn.sigmoid(SWIGLU_ALPHA * glu) * (lin + 1.0)
        y = jnp.dot(wdT_ref[0], act.astype(BF16), preferred_element_type=F32) + jnp.tile(bd_ref[0], (1, reps))
        outT_scr[...] += jnp.dot(y.astype(BF16), onehot, preferred_element_type=F32) * comb_row
        return carry

    lax.fori_loop(0, (count + MOE_CAP - 1) // MOE_CAP, one_pass, 0)

    @pl.when(e == pl.num_programs(1) - 1)
    def _():
        o_ref[...] = _rms(x2_ref[...] + outT_scr[...].T, gfin_ref[...])


def _moe(h2T, combT, x2, w_up, b_up, w_down, b_down, norm_final, tm):
    t, d = x2.shape
    n_experts, _, two_de = w_up.shape
    de = two_de // 2
    wuT = jnp.swapaxes(w_up, 1, 2).astype(BF16)
    wdT = jnp.swapaxes(w_down, 1, 2).astype(BF16)
    lane_copies = lambda bias: jnp.broadcast_to(bias[:, :, None], (*bias.shape, LANES))
    assert MOE_CAP % LANES == 0 and tm % LANES == 0
    return pl.pallas_call(
        functools.partial(_moe_kernel, n_experts=n_experts),
        grid=(t // tm, n_experts),
        in_specs=[pl.BlockSpec((d, tm), lambda i, e: (0, i)),
                  pl.BlockSpec((LANES, tm), lambda i, e: (0, i)),
                  pl.BlockSpec((tm, d), lambda i, e: (i, 0)),
                  pl.BlockSpec((1, two_de, d), lambda i, e: (e, 0, 0)),
                  pl.BlockSpec((1, two_de, LANES), lambda i, e: (e, 0, 0)),
                  pl.BlockSpec((1, d, de), lambda i, e: (e, 0, 0)),
                  pl.BlockSpec((1, d, LANES), lambda i, e: (e, 0, 0)),
                  pl.BlockSpec((1, d), lambda i, e: (0, 0))],
        out_specs=pl.BlockSpec((tm, d), lambda i, e: (i, 0)),
        out_shape=jax.ShapeDtypeStruct((t, d), F32),
        scratch_shapes=[pltpu.VMEM((d, tm), F32), pltpu.VMEM((n_experts, 8, tm), F32),
                        pltpu.VMEM((n_experts, 8, tm), F32)],
        compiler_params=_cparams(("arbitrary", "arbitrary")),
        name="moe",
    )(h2T, combT, x2, wuT, lane_copies(b_up), wdT, lane_copies(b_down), norm_final.reshape(1, d))


NEW_PAD = LANES
PAGES_PER_CHUNK = 8


def _sample_scores_kernel(pt_ref, qi_ref, w_ref, kin_ref, cki_ref, o_ref, buf, sem, *, n_pages, n_new):
    b = pl.program_id(0)
    slot = b % 2

    def page_copy(seq, p, sl):
        return pltpu.make_async_copy(cki_ref.at[pt_ref[seq, p]], buf.at[sl, p], sem.at[sl])

    @pl.when(b == 0)
    def _():
        for p in range(n_pages):
            page_copy(0, p, 0).start()

    @pl.when(b + 1 < pl.num_programs(0))
    def _():
        for p in range(n_pages):
            page_copy(b + 1, p, 1 - slot).start()

    for p in range(n_pages):
        page_copy(b, p, slot).wait()

    nt = (((1,), (1,)), ((), ()))
    qi = qi_ref[0]
    w = w_ref[0]
    half = IDX_HEADS * n_new // 2

    def reduce_heads(s, reps):
        wa = jnp.tile(w[0:half], (1, reps))
        wb = jnp.tile(w[half:2 * half], (1, reps))
        y = jnp.maximum(s[0:half], 0.0) * wa + jnp.maximum(s[half:2 * half], 0.0) * wb
        return (y + pltpu.roll(y, n_new, 0))[0:n_new]

    for c in range(n_pages // PAGES_PER_CHUNK):
        s = jnp.concatenate([jnp.dot(qi, buf[slot, c * PAGES_PER_CHUNK + p].astype(BF16), preferred_element_type=F32)
                             for p in range(PAGES_PER_CHUNK)], axis=1)
        o_ref[0, :, c * PAGES_PER_CHUNK * PAGE_SIZE:(c + 1) * PAGES_PER_CHUNK * PAGE_SIZE] = reduce_heads(
            s, PAGES_PER_CHUNK)
    s = lax.dot_general(qi, kin_ref[0], nt, preferred_element_type=F32)
    y = reduce_heads(s, 1)
    j = lax.broadcasted_iota(jnp.int32, (n_new, NEW_PAD), 1)
    t = lax.broadcasted_iota(jnp.int32, (n_new, NEW_PAD), 0)
    o_ref[0, :, n_pages * PAGE_SIZE:] = jnp.where(j <= t, y, -jnp.inf)


def _sample_scores(page_table, qi_rows, w_rows, ki_new_pad, cache_ki_t):
    nb, n_pages = page_table.shape
    n_new = qi_rows.shape[1] // IDX_HEADS
    assert n_new == 4 and n_pages % PAGES_PER_CHUNK == 0 and cache_ki_t.shape[1:] == (IDX_DIM, PAGE_SIZE)
    lk = n_pages * PAGE_SIZE + NEW_PAD
    grid_spec = pltpu.PrefetchScalarGridSpec(
        num_scalar_prefetch=1,
        grid=(nb,),
        in_specs=[pl.BlockSpec((1, IDX_HEADS * n_new, IDX_DIM), lambda b, pt: (b, 0, 0)),
                  pl.BlockSpec((1, IDX_HEADS * n_new, LANES), lambda b, pt: (b, 0, 0)),
                  pl.BlockSpec((1, NEW_PAD, IDX_DIM), lambda b, pt: (b, 0, 0)),
                  pl.BlockSpec(memory_space=pl.ANY)],
        out_specs=pl.BlockSpec((1, n_new, lk), lambda b, pt: (b, 0, 0)),
        scratch_shapes=[pltpu.VMEM((2, n_pages, IDX_DIM, PAGE_SIZE), F32), pltpu.SemaphoreType.DMA((2,))],
    )
    return pl.pallas_call(
        functools.partial(_sample_scores_kernel, n_pages=n_pages, n_new=n_new),
        grid_spec=grid_spec,
        out_shape=jax.ShapeDtypeStruct((nb, n_new, lk), F32),
        compiler_params=_cparams(("arbitrary",)),
        name="sample_scores",
    )(page_table, qi_rows, w_rows, ki_new_pad, cache_ki_t)


def _sample_select_kernel(s_ref, sel_ref, *, topk, n_past, n_new):
    rows, lk = s_ref.shape
    key = _canonical_zero(s_ref[...])

    def bisect(step, tkey):
        cand = tkey + lax.shift_left(jnp.int32(1), 31 - step)
        n_ge = jnp.sum(jnp.where(key >= _key_to_float(cand), 1.0, 0.0), axis=1, keepdims=True)
        return jnp.where(n_ge >= topk, cand, tkey)

    thr = _key_to_float(lax.fori_loop(0, 32, bisect, jnp.full((rows, 1), INT_MIN, jnp.int32)))
    n_take = topk - jnp.sum(jnp.where(key > thr, 1.0, 0.0), axis=1, keepdims=True)
    col = lax.broadcasted_iota(jnp.int32, (rows, lk), 1)
    t = lax.broadcasted_iota(jnp.int32, (rows, lk), 0) % n_new
    visible = col - n_past <= t
    eq = jnp.where(key == thr, 1.0, 0.0)
    ra = lax.broadcasted_iota(jnp.int32, (LANES, LANES), 0)
    ca = lax.broadcasted_iota(jnp.int32, (LANES, LANES), 1)
    upper = jnp.where(ra < ca, 1.0, 0.0).astype(BF16)
    seen = jnp.zeros((rows, 1), F32)
    for c in range(lk // LANES):
        sl = slice(c * LANES, (c + 1) * LANES)
        eqc = eq[:, sl]
        before = seen + jnp.dot(eqc.astype(BF16), upper, preferred_element_type=F32)
        take = jnp.where(key[:, sl] > thr, 1.0, jnp.where(before < n_take, eqc, 0.0))
        sel_ref[:, sl] = jnp.where(visible[:, sl], take, 0.0)
        seen = seen + jnp.sum(eqc, axis=1, keepdims=True)


def _sample_select(scores2d, topk, n_past, n_new, rows):
    n, lk = scores2d.shape
    assert n % rows == 0 and rows % n_new == 0 and lk == n_past + NEW_PAD
    return pl.pallas_call(
        functools.partial(_sample_select_kernel, topk=topk, n_past=n_past, n_new=n_new),
        grid=(n // rows,),
        in_specs=[pl.BlockSpec((rows, lk), lambda i: (i, 0))],
        out_specs=pl.BlockSpec((rows, lk), lambda i: (i, 0)),
        out_shape=jax.ShapeDtypeStruct((n, lk), F32),
        compiler_params=_cparams(("arbitrary",)),
        name="sample_select",
    )(scores2d)


def _sample_attend_kernel(pt_ref, q_ref, sel_ref, bias_ref, kn_ref, vn_ref, ck_ref, cv_ref, o_ref,
                          kbuf, vbuf, ksem, vsem, *, n_pages, n_new):
    b = pl.program_id(0)
    n_chunks = n_pages // PAGES_PER_CHUNK
    keys_per_chunk = PAGES_PER_CHUNK * PAGE_SIZE
    rows = n_new * A_HEADS

    def chunk_copies(seq, c, sl):
        cps = []
        for p in range(PAGES_PER_CHUNK):
            page = pt_ref[seq, c * PAGES_PER_CHUNK + p]
            cps.append(pltpu.make_async_copy(ck_ref.at[page], kbuf.at[sl, p], ksem.at[sl]))
            cps.append(pltpu.make_async_copy(cv_ref.at[page], vbuf.at[sl, p], vsem.at[sl]))
        return cps

    @pl.when(b == 0)
    def _():
        for cp in chunk_copies(0, 0, 0):
            cp.start()

    hrow = lax.broadcasted_iota(jnp.int32, (A_HEADS, A_WIDTH), 0)
    hcol = lax.broadcasted_iota(jnp.int32, (A_HEADS, A_WIDTH), 1) // A_HEAD_DIM
    head_mask = jnp.where(hrow == hcol, 1.0, 0.0)
    qf = q_ref[0].astype(F32)
    qbd = jnp.concatenate([qf[t:t + 1, :] * head_mask for t in range(n_new)], axis=0).astype(BF16)
    nt = (((1,), (1,)), ((), ()))

    def softmax_step(state, logits, lanes, weighted_values):
        m_old, l_old, acc = state
        sel = sel_ref[0, :, lanes]
        sel_rows = jnp.concatenate([jnp.broadcast_to(sel[t:t + 1, :], (A_HEADS, sel.shape[1])) for t in range(n_new)],
                                   axis=0)
        x = jnp.where(sel_rows > 0.5, logits + bias_ref[:, lanes], NEG_BIG)
        m_new = jnp.maximum(m_old, jnp.max(x, axis=1, keepdims=True))
        alpha = jnp.exp(m_old - m_new)
        pm = jnp.exp(x - m_new)
        l_new = alpha * l_old + jnp.sum(pm, axis=1, keepdims=True)
        return m_new, l_new, alpha * acc + weighted_values(pm.astype(BF16))

    def chunk_body(c, state):
        sl = c % 2

        @pl.when(c + 1 < n_chunks)
        def _():
            for cp in chunk_copies(b, c + 1, 1 - sl):
                cp.start()

        @pl.when((c + 1 == n_chunks) & (b + 1 < pl.num_programs(0)))
        def _():
            for cp in chunk_copies(b + 1, 0, 1 - sl):
                cp.start()

        for cp in chunk_copies(b, c, sl):
            cp.wait()
        logits = jnp.concatenate([jnp.dot(qbd, kbuf[sl, p].astype(BF16), preferred_element_type=F32)
                                  for p in range(PAGES_PER_CHUNK)], axis=1)

        def weighted_values(pm):
            acc = jnp.zeros((rows, A_WIDTH), F32)
            for p in range(PAGES_PER_CHUNK):
                acc = acc + lax.dot_general(pm[:, p * PAGE_SIZE:(p + 1) * PAGE_SIZE], vbuf[sl, p].astype(BF16), nt,
                                            preferred_element_type=F32)
            return acc

        lanes = pl.ds(pl.multiple_of(c * keys_per_chunk, keys_per_chunk), keys_per_chunk)
        return softmax_step(state, logits, lanes, weighted_values)

    state = (jnp.full((rows, 1), NEG_BIG, F32), jnp.zeros((rows, 1), F32), jnp.zeros((rows, A_WIDTH), F32))
    state = lax.fori_loop(0, n_chunks, chunk_body, state)
    logits = lax.dot_general(qbd, kn_ref[0], nt, preferred_element_type=F32)
    _, l_fin, acc = softmax_step(state, logits, slice(n_pages * PAGE_SIZE, n_pages * PAGE_SIZE + NEW_PAD),
                                 lambda pm: jnp.dot(pm, vn_ref[0], preferred_element_type=F32))
    full_mask = jnp.concatenate([head_mask] * n_new, axis=0)
    o = (acc / l_fin) * full_mask
    o_ref[0] = jnp.sum(o.reshape(n_new, A_HEADS, A_WIDTH), axis=1).astype(o_ref.dtype)


def _sample_attend(page_table, q_bf, sel, bias_rows, k_new_pad, v_new_pad, cache_k_t, cache_v_t):
    nb, n_pages = page_table.shape
    n_new = q_bf.shape[1]
    assert n_pages % (2 * PAGES_PER_CHUNK) == 0
    assert cache_k_t.shape[1:] == (A_WIDTH, PAGE_SIZE)
    lk = n_pages * PAGE_SIZE + NEW_PAD
    page_buf = pltpu.VMEM((2, PAGES_PER_CHUNK, A_WIDTH, PAGE_SIZE), F32)
    grid_spec = pltpu.PrefetchScalarGridSpec(
        num_scalar_prefetch=1,
        grid=(nb,),
        in_specs=[pl.BlockSpec((1, n_new, A_WIDTH), lambda b, pt: (b, 0, 0)),
                  pl.BlockSpec((1, n_new, lk), lambda b, pt: (b, 0, 0)),
                  pl.BlockSpec((n_new * A_HEADS, lk), lambda b, pt: (0, 0), pipeline_mode=pl.Buffered(1)),
                  pl.BlockSpec((1, NEW_PAD, A_WIDTH), lambda b, pt: (b, 0, 0)),
                  pl.BlockSpec((1, NEW_PAD, A_WIDTH), lambda b, pt: (b, 0, 0)),
                  pl.BlockSpec(memory_space=pl.ANY),
                  pl.BlockSpec(memory_space=pl.ANY)],
        out_specs=pl.BlockSpec((1, n_new, A_WIDTH), lambda b, pt: (b, 0, 0)),
        scratch_shapes=[page_buf, page_buf, pltpu.SemaphoreType.DMA((2,)), pltpu.SemaphoreType.DMA((2,))],
    )
    return pl.pallas_call(
        functools.partial(_sample_attend_kernel, n_pages=n_pages, n_new=n_new),
        grid_spec=grid_spec,
        out_shape=jax.ShapeDtypeStruct((nb, n_new, A_WIDTH), BF16),
        compiler_params=_cparams(("arbitrary",)),
        name="sample_attend",
    )(page_table, q_bf, sel, bias_rows, k_new_pad, v_new_pad, cache_k_t, cache_v_t)


def _dsa_sample(qa_bf, kab, vab, qi_bf, kiw, rel_bias, cache_k, cache_v, cache_ki, page_table, n_new):
    nb, n_pages = page_table.shape
    n_past = n_pages * PAGE_SIZE
    assert n_new <= NEW_PAD and n_past >= REL_MAX_DIST
    topk = min(INDEX_TOPK, (n_past + n_new) // 4)
    qi_rows = qi_bf.reshape(nb, n_new, IDX_HEADS, IDX_DIM).transpose(0, 2, 1, 3).reshape(nb, IDX_HEADS * n_new, IDX_DIM)
    wi = kiw[:, IDX_DIM:IDX_DIM + IDX_HEADS] * INDEX_SCALE
    w_rows = jnp.broadcast_to(wi.reshape(nb, n_new, IDX_HEADS).transpose(0, 2, 1).reshape(nb, IDX_HEADS * n_new, 1),
                              (nb, IDX_HEADS * n_new, LANES))
    pad_rows = lambda a: jnp.pad(a, ((0, 0), (0, NEW_PAD - a.shape[1]), (0, 0)))
    ki_new = pad_rows(kiw[:, :IDX_DIM].astype(BF16).reshape(nb, n_new, IDX_DIM))
    n_pool = cache_k.shape[0]
    cache_ki_t = cache_ki.transpose(0, 2, 1)
    cache_k_t = cache_k.transpose(0, 2, 3, 1).reshape(n_pool, A_WIDTH, PAGE_SIZE)
    cache_v_t = cache_v.transpose(0, 2, 3, 1).reshape(n_pool, A_WIDTH, PAGE_SIZE)
    scores = _sample_scores(page_table, qi_rows, w_rows, ki_new, cache_ki_t)
    lk = scores.shape[-1]
    sel = _sample_select(scores.reshape(nb * n_new, lk), topk, n_past, n_new, rows=32)
    n_tail = REL_MAX_DIST + NEW_PAD
    table = _bias_by_distance(rel_bias, REL_MAX_DIST + n_tail)
    dist = (n_past + jnp.arange(n_new))[:, None] - (n_past - REL_MAX_DIST + jnp.arange(n_tail))[None, :]
    tail = table[:, jnp.clip(dist, 0, table.shape[1] - 1)].transpose(1, 0, 2)
    far = jnp.broadcast_to(table[None, :, -1:], (n_new, A_HEADS, n_past - REL_MAX_DIST))
    bias_rows = jnp.concatenate([far, tail], axis=2).reshape(n_new * A_HEADS, lk)
    o = _sample_attend(page_table, qa_bf.reshape(nb, n_new, A_WIDTH), sel.reshape(nb, n_new, lk), bias_rows,
                       pad_rows(kab.reshape(nb, n_new, A_WIDTH)), pad_rows(vab.reshape(nb, n_new, A_WIDTH)),
                       cache_k_t, cache_v_t)
    return o.reshape(nb * n_new, A_WIDTH)


def kernel(x_prompt, x_sample, cache_k, cache_v, cache_idx_k, state_gla, page_table, norm_mix, w_in, w_gate_up, b_gate, gla_norm, w_branch_a, w_branch_b, w_out, norm_ffn, w_router, b_router, w_exp_up, b_exp_up, w_exp_down, b_exp_down, rel_bias, norm_final):
    batch, seq, d = x_prompt.shape
    nb, n_new, _ = x_sample.shape
    assert w_in.shape[0] == 1, "the final norm is fused into the single layer's MoE kernel"
    layer = 0
    w_packed = _pack_w_in(w_in[layer])

    def trunk(x2d, attend, gla_args, tm_moe):
        tokens = x2d.shape[0]
        tm = min(512, tokens)
        qa, ka, kab, va, vab, qi, kiw, qb, kb, vb, r, glow, ga, gb = _inproj(x2d, norm_mix[layer], w_packed, tm)
        oa = attend(qa, kab, vab, qi, kiw)
        ob, state = gla_args(qb, kb, vb, r, glow)
        x2, h2, comb = _merge(oa, ob, ga, gb, x2d, w_branch_a[layer], w_branch_b[layer], w_out[layer], norm_ffn[layer],
                              w_router[layer], b_router[layer], tm)
        y = _moe(h2, comb, x2, w_exp_up[layer], b_exp_up[layer], w_exp_down[layer], b_exp_down[layer], norm_final,
                 min(tm_moe, tokens))
        return y, ka, va, kiw[:, :IDX_DIM], state

    gla_w = (w_gate_up[layer], b_gate[layer], gla_norm[layer])

    def gla_prompt(qb, kb, vb, r, glow):
        s0 = jnp.zeros((batch, B_HEADS, B_KEY_DIM, B_VAL_DIM), F32)
        return _gla(qb, kb, vb, r, glow, *gla_w, s0, batch, seq, GLA_CHUNK)

    def gla_sample(qb, kb, vb, r, glow):
        pad = lambda a: jnp.pad(a.reshape(nb, n_new, -1), ((0, 0), (0, GLA_CHUNK - n_new), (0, 0))).reshape(nb * GLA_CHUNK, -1)
        o, state = _gla(pad(qb), pad(kb), pad(vb), pad(r), pad(glow), *gla_w, state_gla[layer].astype(F32), nb, GLA_CHUNK,
                        n_new)
        return o.reshape(nb, GLA_CHUNK, -1)[:, :n_new].reshape(nb * n_new, -1), state

    yp, kp, vp, kip, sp = trunk(
        x_prompt.reshape(batch * seq, d),
        lambda qa, kab, vab, qi, kiw: _dsa_prompt(qa, kab, vab, qi, kiw, rel_bias, batch, seq),
        gla_prompt, 1024)
    ys, ks, vs, kis, ss = trunk(
        x_sample.reshape(nb * n_new, d),
        lambda qa, kab, vab, qi, kiw: _dsa_sample(qa, kab, vab, qi, kiw, rel_bias, cache_k[layer], cache_v[layer],
                                                  cache_idx_k[layer], page_table, n_new),
        gla_sample, 512)
    heads = (A_HEADS, A_HEAD_DIM)
    return (yp.reshape(batch, seq, d), ys.reshape(nb, n_new, d),
            kp.reshape(1, batch, seq, *heads), vp.reshape(1, batch, seq, *heads), kip.reshape(1, batch, seq, IDX_DIM),
            sp.astype(x_prompt.dtype)[None],
            ks.reshape(1, nb, n_new, *heads), vs.reshape(1, nb, n_new, *heads), kis.reshape(1, nb, n_new, IDX_DIM),
            ss.astype(state_gla.dtype)[None])
```
